```python
import math
import jax, jax.numpy as jnp
from jax import lax
import numpy as np

D_MODEL = 1024
BATCH = 8
SEQ = 2048
DEPTH = 1
DEC_BATCH = 128
DEC_SEQ = 1
PAST_LEN = 8192
PAGE_SIZE = 128

N_HEADS = 8
HEAD_DIM = 64
ATTN_W = N_HEADS * HEAD_DIM
CONV_CH = D_MODEL // 2
CONV_WIDTH = 3
D_FF = ((8 * D_MODEL // 3 + 255) // 256) * 256
FFN_CONV_WIDTH = 3
BLOCK = 256
TOP_K = 3
Q_BLOCK = 128
NUM_BUCKETS = 32
MAX_DISTANCE = 128
EPS = 1e-6
SPLITS = [ATTN_W, 2 * ATTN_W, 3 * ATTN_W, 3 * ATTN_W + CONV_CH, 3 * ATTN_W + 2 * CONV_CH,
          3 * ATTN_W + 3 * CONV_CH, 3 * ATTN_W + 3 * CONV_CH + D_MODEL]
N_COLS = 3 * ATTN_W + 3 * CONV_CH + 2 * D_MODEL

kernel_name = 'moba_shortconv_gated_hybrid_step'


def rmsnorm(x, w):
    x32 = x.astype(jnp.float32)
    y = x32 * lax.rsqrt(jnp.mean(x32 * x32, axis=-1, keepdims=True) + EPS) * w.astype(jnp.float32)
    return y.astype(x.dtype)


def t5_bucket(dist):
    dist = jnp.maximum(dist, 0)
    max_exact = NUM_BUCKETS // 2
    logd = jnp.log(jnp.maximum(dist, 1).astype(jnp.float32) / max_exact)
    large = max_exact + (logd / math.log(MAX_DISTANCE / max_exact) * (NUM_BUCKETS - max_exact)).astype(jnp.int32)
    return jnp.where(dist < max_exact, dist, jnp.minimum(large, NUM_BUCKETS - 1))


def causal_dwconv(u, prefix, w):
    t = u.shape[1]
    ue = jnp.concatenate([prefix.astype(u.dtype), u], axis=1)
    out = ue[:, 0:t] * w[0]
    for j in range(1, w.shape[0]):
        out = out + ue[:, j:j + t] * w[j]
    return out, ue[:, t:]


def moba_core(q, qpos, k_sel, v_sel, sel_kpos, sel_ok, k_own, v_own, own_kpos, rel_bias):
    hid = jnp.arange(q.shape[1])
    l_sel = jnp.einsum('nhd,nhrd->nhr', q, k_sel).astype(jnp.float32)
    b_sel = rel_bias[t5_bucket(qpos[:, None, None] - sel_kpos), hid[None, :, None]].astype(jnp.float32)
    l_sel = jnp.where(sel_ok, l_sel + b_sel, -jnp.inf)
    l_own = jnp.einsum('nhd,mhd->nhm', q, k_own).astype(jnp.float32)
    b_own = jnp.transpose(rel_bias[t5_bucket(qpos[:, None] - own_kpos[None, :])], (0, 2, 1)).astype(jnp.float32)
    causal = (own_kpos[None, :] <= qpos[:, None])[:, None, :]
    l_own = jnp.where(causal, l_own + b_own, -jnp.inf)
    p = jax.nn.softmax(jnp.concatenate([l_sel, l_own], axis=-1), axis=-1).astype(v_own.dtype)
    r = l_sel.shape[-1]
    return (jnp.einsum('nhr,nhrd->nhd', p[..., :r], v_sel)
            + jnp.einsum('nhm,mhd->nhd', p[..., r:], v_own))


def moba_prompt(q, k, v, rel_bias):
    b, s, h, hd = q.shape
    nb = -(-s // BLOCK)
    pad = nb * BLOCK - s
    kp = jnp.pad(k, ((0, 0), (0, pad), (0, 0), (0, 0))).reshape(b, nb, BLOCK, h, hd)
    vp = jnp.pad(v, ((0, 0), (0, pad), (0, 0), (0, 0))).reshape(b, nb, BLOCK, h, hd)
    kmean = kp.mean(axis=2)
    qblk = jnp.arange(s) // BLOCK
    blk = jnp.arange(nb)
    gate = jnp.einsum('bshd,bnhd->bshn', q, kmean).astype(jnp.float32)
    gate = jnp.where(blk[None, None, None, :] < qblk[None, :, None, None], gate, -jnp.inf)
    ksel = min(TOP_K, nb)
    _, idx = lax.top_k(gate, ksel)
    valid = jnp.broadcast_to(jnp.arange(ksel)[None, None, None, :] < qblk[None, :, None, None], idx.shape)
    nqc = s // Q_BLOCK
    n_items = b * nqc
    q_it = q.reshape(n_items, Q_BLOCK, h, hd)
    idx_it = idx.reshape(n_items, Q_BLOCK, h, ksel)
    ok_it = valid.reshape(n_items, Q_BLOCK, h, ksel)
    b_it = jnp.repeat(jnp.arange(b), nqc)
    c_it = jnp.tile(jnp.arange(nqc), b)
    hid4 = jnp.arange(h)[None, :, None, None]
    offs = jnp.arange(BLOCK)
    r = ksel * BLOCK

    def one(args):
        qc, ic, okc, bi, ci = args
        kb, vb = kp[bi], vp[bi]
        gk = kb[ic[..., None], offs, hid4].reshape(Q_BLOCK, h, r, hd)
        gv = vb[ic[..., None], offs, hid4].reshape(Q_BLOCK, h, r, hd)
        sel_kpos = (ic[..., None] * BLOCK + offs).reshape(Q_BLOCK, h, r)
        sel_ok = jnp.broadcast_to(okc[..., None], (Q_BLOCK, h, ksel, BLOCK)).reshape(Q_BLOCK, h, r)
        qpos = ci * Q_BLOCK + jnp.arange(Q_BLOCK)
        ob = (ci * Q_BLOCK) // BLOCK
        own_kpos = ob * BLOCK + offs
        return moba_core(qc, qpos, gk, gv, sel_kpos, sel_ok, kb[ob], vb[ob], own_kpos, rel_bias)

    out = lax.map(one, (q_it, idx_it, ok_it, b_it, c_it))
    return out.reshape(b, s, h, hd)


def moba_sample(q, k_new, v_new, cache_k, cache_v, layer, page_table, rel_bias):
    db, t, h, hd = q.shape
    page = cache_k.shape[2]
    n_pages = page_table.shape[1]
    past = n_pages * page
    ppb = BLOCK // page
    nf = past // BLOCK
    s0 = nf * BLOCK
    n_own_past = past - s0
    own_pages = page_table[:, s0 // page:]
    k_own = jnp.concatenate([cache_k[layer, own_pages].reshape(db, n_own_past, h, hd), k_new], axis=1)
    v_own = jnp.concatenate([cache_v[layer, own_pages].reshape(db, n_own_past, h, hd), v_new], axis=1)
    own_kpos = s0 + jnp.arange(n_own_past + t)
    qpos = past + jnp.arange(t)
    if nf > 0:
        rows = cache_k[layer, page_table[:, :nf * ppb]]
        kmean = rows.reshape(db, nf, BLOCK, h, hd).mean(axis=2)
        gate = jnp.einsum('bthd,bnhd->bthn', q, kmean).astype(jnp.float32)
        ksel = min(TOP_K, nf)
        _, idx = lax.top_k(gate, ksel)
        logical = idx[..., None] * ppb + jnp.arange(ppb)
        phys = page_table[jnp.arange(db)[:, None, None, None, None], logical]
        hid6 = jnp.arange(h)[None, None, :, None, None, None]
        r = ksel * BLOCK
        gk = cache_k[layer, phys[..., None], jnp.arange(page), hid6].reshape(db, t, h, r, hd)
        gv = cache_v[layer, phys[..., None], jnp.arange(page), hid6].reshape(db, t, h, r, hd)
        sel_kpos = (idx[..., None] * BLOCK + jnp.arange(BLOCK)).reshape(db, t, h, r)
    else:
        r = 0
        gk = jnp.zeros((db, t, h, 0, hd), q.dtype)
        gv = jnp.zeros((db, t, h, 0, hd), q.dtype)
        sel_kpos = jnp.zeros((db, t, h, 0), jnp.int32)
    sel_ok = jnp.ones((db, t, h, r), dtype=bool)
    core = jax.vmap(moba_core, in_axes=(0, None, 0, 0, 0, 0, 0, 0, None, None))
    return core(q, qpos, gk, gv, sel_kpos, sel_ok, k_own, v_own, own_kpos, rel_bias)


def mixer_inputs(x, n_w, w_in, qn_w, kn_w):
    b, t, _ = x.shape
    p = rmsnorm(x, n_w) @ w_in
    q, k, v, cb, cc, ch, ga, gb = jnp.split(p, SPLITS, axis=-1)
    q = rmsnorm(q.reshape(b, t, N_HEADS, HEAD_DIM), qn_w) * (HEAD_DIM ** -0.5)
    k = rmsnorm(k.reshape(b, t, N_HEADS, HEAD_DIM), kn_w)
    v = v.reshape(b, t, N_HEADS, HEAD_DIM)
    return q, k, v, cb, cc * ch, ga, gb


def mixer_merge(x, attn, cb, u, ga, gb, conv_prefix, conv_w, w_attn_up, w_conv_out, w_o):
    b, t, _ = x.shape
    uc, conv_state = causal_dwconv(u, conv_prefix, conv_w)
    branch_a = attn.reshape(b, t, ATTN_W) @ w_attn_up
    branch_b = (cb * uc) @ w_conv_out
    merged = jax.nn.sigmoid(ga) * branch_a + jax.nn.sigmoid(gb) * branch_b
    return x + merged @ w_o, conv_state


def conv_glu(x, prefix, n_w, w_up, conv_w, w_down):
    g, u = jnp.split(rmsnorm(x, n_w) @ w_up, 2, axis=-1)
    gc, state = causal_dwconv(g, prefix, conv_w)
    return x + (jax.nn.silu(gc) * u) @ w_down, state


def setup_inputs(seed: int = 0) -> dict:
    key = jax.random.key(seed)
    ks = jax.random.split(key, 20)
    n_pages = PAST_LEN // PAGE_SIZE
    n_used = DEC_BATCH * n_pages
    n_pool = (5 * n_used + 3) // 4
    f32 = jnp.float32

    def nrm(k, shape, scale=1.0):
        return jax.random.normal(k, shape, f32) * scale

    page_table = jax.random.permutation(ks[4], n_pool)[:n_used].reshape(DEC_BATCH, n_pages).astype(jnp.int32)
    return {
        'x_prompt': nrm(ks[0], (BATCH, SEQ, D_MODEL)),
        'x_sample': nrm(ks[1], (DEC_BATCH, DEC_SEQ, D_MODEL)),
        'cache_k': nrm(ks[2], (DEPTH, n_pool, PAGE_SIZE, N_HEADS, HEAD_DIM)),
        'cache_v': nrm(ks[3], (DEPTH, n_pool, PAGE_SIZE, N_HEADS, HEAD_DIM)),
        'page_table': page_table,
        'state_conv': nrm(ks[5], (DEPTH, DEC_BATCH, CONV_WIDTH - 1, CONV_CH)),
        'state_ffn': nrm(ks[6], (DEPTH, DEC_BATCH, FFN_CONV_WIDTH - 1, D_FF)),
        'norm1_w': 1.0 + nrm(ks[7], (DEPTH, D_MODEL), 0.02),
        'w_in': nrm(ks[8], (DEPTH, D_MODEL, N_COLS), D_MODEL ** -0.5),
        'q_norm_w': 1.0 + nrm(ks[9], (DEPTH, HEAD_DIM), 0.02),
        'k_norm_w': 1.0 + nrm(ks[10], (DEPTH, HEAD_DIM), 0.02),
        'conv_w': nrm(ks[11], (DEPTH, CONV_WIDTH, CONV_CH), CONV_WIDTH ** -0.5),
        'w_attn_up': nrm(ks[12], (DEPTH, ATTN_W, D_MODEL), ATTN_W ** -0.5),
        'w_conv_out': nrm(ks[13], (DEPTH, CONV_CH, D_MODEL), CONV_CH ** -0.5),
        'w_o': nrm(ks[14], (DEPTH, D_MODEL, D_MODEL), D_MODEL ** -0.5),
        'norm2_w': 1.0 + nrm(ks[15], (DEPTH, D_MODEL), 0.02),
        'w_ffn_up': nrm(ks[16], (DEPTH, D_MODEL, 2 * D_FF), D_MODEL ** -0.5),
        'ffn_conv_w': nrm(ks[17], (DEPTH, FFN_CONV_WIDTH, D_FF), FFN_CONV_WIDTH ** -0.5),
        'w_ffn_down': nrm(ks[18], (DEPTH, D_FF, D_MODEL), D_FF ** -0.5),
        'rel_bias': nrm(ks[19], (NUM_BUCKETS, N_HEADS), 0.5),
    }


def reference(x_prompt, x_sample, cache_k, cache_v, page_table, state_conv, state_ffn,
              norm1_w, w_in, q_norm_w, k_norm_w, conv_w, w_attn_up, w_conv_out, w_o,
              norm2_w, w_ffn_up, ffn_conv_w, w_ffn_down, rel_bias):
    xp, xs = x_prompt, x_sample
    kp_l, vp_l, cp_l, fp_l = [], [], [], []
    ks_l, vs_l, cs_l, fs_l = [], [], [], []
    for l in range(DEPTH):
        q, k, v, cb, u, ga, gb = mixer_inputs(xp, norm1_w[l], w_in[l], q_norm_w[l], k_norm_w[l])
        attn = moba_prompt(q, k, v, rel_bias)
        zc = jnp.zeros((xp.shape[0], CONV_WIDTH - 1, CONV_CH), xp.dtype)
        xp, cst = mixer_merge(xp, attn, cb, u, ga, gb, zc, conv_w[l], w_attn_up[l], w_conv_out[l], w_o[l])
        zf = jnp.zeros((xp.shape[0], FFN_CONV_WIDTH - 1, D_FF), xp.dtype)
        xp, fst = conv_glu(xp, zf, norm2_w[l], w_ffn_up[l], ffn_conv_w[l], w_ffn_down[l])
        kp_l.append(k); vp_l.append(v); cp_l.append(cst); fp_l.append(fst)
        q, k, v, cb, u, ga, gb = mixer_inputs(xs, norm1_w[l], w_in[l], q_norm_w[l], k_norm_w[l])
        attn = moba_sample(q, k, v, cache_k, cache_v, l, page_table, rel_bias)
        xs, cst = mixer_merge(xs, attn, cb, u, ga, gb, state_conv[l], conv_w[l], w_attn_up[l], w_conv_out[l], w_o[l])
        xs, fst = conv_glu(xs, state_ffn[l], norm2_w[l], w_ffn_up[l], ffn_conv_w[l], w_ffn_down[l])
        ks_l.append(k); vs_l.append(v); cs_l.append(cst); fs_l.append(fst)
    return (xp, xs,
            jnp.stack(kp_l), jnp.stack(vp_l), jnp.stack(cp_l), jnp.stack(fp_l),
            jnp.stack(ks_l), jnp.stack(vs_l), jnp.stack(cs_l), jnp.stack(fs_l))
```

```python
import functools
import math

import numpy as np
import jax
import jax.numpy as jnp
from jax import lax
from jax.experimental import pallas as pl
from jax.experimental.pallas import tpu as pltpu

F32 = jnp.float32
BF16 = jnp.bfloat16

D_MODEL = 1024
N_HEADS = 8
HEAD_DIM = 64
ATTN_W = N_HEADS * HEAD_DIM
CONV_CH = D_MODEL // 2
D_FF = 2816
BLOCK = 256
TOP_K = 3
PAGE = 128
NUM_BUCKETS = 32
MAX_DISTANCE = 128
EPS = 1e-6
N_COLS = 3 * ATTN_W + 3 * CONV_CH + 2 * D_MODEL

FF_CHUNK = 256
N_FF_CHUNKS = D_FF // FF_CHUNK
TOKEN_TILE = 512
PAGES_PER_STEP = 16
NEG_BIG = -1e30
VMEM_LIMIT = 60 * 1024 * 1024


def _bucket_thresholds():
    max_exact = NUM_BUCKETS // 2
    d = np.arange(0, 4 * MAX_DISTANCE)
    logd = np.log(np.maximum(d, 1) / max_exact)
    large = max_exact + (logd / math.log(MAX_DISTANCE / max_exact) * (NUM_BUCKETS - max_exact)).astype(np.int32)
    bucket = np.where(d < max_exact, d, np.minimum(large, NUM_BUCKETS - 1))
    return [int(np.argmax(bucket >= b)) for b in range(NUM_BUCKETS)]


BUCKET_THR = _bucket_thresholds()


def _dot(a, b):
    return jnp.dot(a, b, preferred_element_type=F32)


def _dot_nt(a, b):
    return lax.dot_general(a, b, (((1,), (1,)), ((), ())), preferred_element_type=F32)


def _const_spec(shape):
    nd = len(shape)
    return pl.BlockSpec(shape, lambda *_: (0,) * nd, pipeline_mode=pl.Buffered(1))


def _bias_from_distance(d, rb_ref, h, b_far):
    val = jnp.zeros(d.shape, F32)
    for b in range(NUM_BUCKETS - 2, -1, -1):
        val = jnp.where(d < BUCKET_THR[b + 1], rb_ref[b, h] - b_far, val)
    return val


def _bias_kernel(rb_ref, tile_ref, row_ref, own_ref):
    h = pl.program_id(0)
    b_far = rb_ref[NUM_BUCKETS - 1, h]
    j = lax.broadcasted_iota(jnp.int32, (BLOCK, BLOCK), 0)
    i = lax.broadcasted_iota(jnp.int32, (BLOCK, BLOCK), 1)
    for t in range(2):
        tile_ref[0, t] = _bias_from_distance(i - j + BLOCK * t, rb_ref, h, b_far)
    jr = lax.broadcasted_iota(jnp.int32, (1, BLOCK), 1)
    row_ref[0] = _bias_from_distance(BLOCK - jr, rb_ref, h, b_far)
    own_ref[0] = jnp.zeros((1, 128), F32) + (rb_ref[0, h] - b_far)


def _bias_tables(rel_bias):
    return pl.pallas_call(
        _bias_kernel,
        grid=(N_HEADS,),
        in_specs=[pl.BlockSpec(memory_space=pltpu.SMEM)],
        out_specs=[
            pl.BlockSpec((1, 2, BLOCK, BLOCK), lambda h: (h, 0, 0, 0)),
            pl.BlockSpec((1, 1, BLOCK), lambda h: (h, 0, 0)),
            pl.BlockSpec((1, 1, 128), lambda h: (h, 0, 0)),
        ],
        out_shape=[
            jax.ShapeDtypeStruct((N_HEADS, 2, BLOCK, BLOCK), F32),
            jax.ShapeDtypeStruct((N_HEADS, 1, BLOCK), F32),
            jax.ShapeDtypeStruct((N_HEADS, 1, 128), F32),
        ],
    )(rel_bias)


def _inproj_kernel(*refs, tm, sample, tiles_per_seq):
    if sample:
        (x_ref, n1_ref, w_ref, qn_ref, kn_ref, bd_ref, cw_ref, st0_ref, st1_ref,
         q_ref, k_ref, v_ref, sin_ref, ga_ref, gb_ref, u_ref) = refs
    else:
        (x_ref, n1_ref, w_ref, qn_ref, kn_ref, bd_ref, cw_ref,
         q_ref, k_ref, v_ref, kb_ref, vb_ref, ksum_ref, sin_ref, ga_ref, gb_ref, cst_ref,
         ucar_ref) = refs

    x = x_ref[...]
    ms = jnp.mean(x * x, axis=-1, keepdims=True)
    xn = (x * lax.rsqrt(ms + EPS) * n1_ref[...]).astype(BF16)

    def proj(a, b):
        return _dot(xn, w_ref[:, a:b])

    def head_norm(t, w_row):
        sq = (t * t).astype(BF16)
        bd = bd_ref[...]
        half = ATTN_W // 2
        msq = jnp.concatenate([_dot(sq[:, :half], bd), _dot(sq[:, half:], bd)], axis=1)
        return t * lax.rsqrt(msq + EPS) * w_row

    q = head_norm(proj(0, ATTN_W), qn_ref[...]) * (HEAD_DIM ** -0.5)
    k = head_norm(proj(ATTN_W, 2 * ATTN_W), kn_ref[...])
    v = proj(2 * ATTN_W, 3 * ATTN_W)
    c0 = 3 * ATTN_W
    cb = proj(c0, c0 + CONV_CH)
    u = proj(c0 + CONV_CH, c0 + 2 * CONV_CH) * proj(c0 + 2 * CONV_CH, c0 + 3 * CONV_CH)
    g0 = c0 + 3 * CONV_CH
    ga_ref[...] = proj(g0, g0 + D_MODEL).astype(ga_ref.dtype)
    gb_ref[...] = proj(g0 + D_MODEL, g0 + 2 * D_MODEL).astype(gb_ref.dtype)

    k_ref[...] = k
    v_ref[...] = v
    w0, w1, w2 = cw_ref[0:1, :], cw_ref[1:2, :], cw_ref[2:3, :]
    if sample:
        q_ref[...] = q
        uc = st0_ref[...] * w0 + st1_ref[...] * w1 + u * w2
        u_ref[...] = u
    else:
        q_ref[...] = q.astype(BF16)
        kb_ref[...] = k.astype(BF16)
        vb_ref[...] = v.astype(BF16)
        for r in range(tm // BLOCK):
            ksum_ref[0, r:r + 1, :] = jnp.sum(k[r * BLOCK:(r + 1) * BLOCK], axis=0, keepdims=True)

        @pl.when(pl.program_id(0) % tiles_per_seq == 0)
        def _():
            ucar_ref[...] = jnp.zeros_like(ucar_ref)

        ue = jnp.concatenate([ucar_ref[...], u], axis=0)
        uc = ue[6:6 + tm] * w0 + ue[7:7 + tm] * w1 + u * w2
        ucar_ref[...] = u[tm - 8:tm]
        cst_ref[0] = u[tm - 2:tm]
    sin_ref[...] = (cb * uc).astype(sin_ref.dtype)


def _inproj(x, n1, w_in, qn, kn, bd, conv_w, states, *, sample, seq_len=1):
    n_tok = x.shape[0]
    tm = n_tok if sample else TOKEN_TILE
    n_tiles = n_tok // tm
    tiles_per_seq = 1 if sample else seq_len // tm

    def row(width):
        return pl.BlockSpec((tm, width), lambda i: (i, 0))

    in_specs = [row(D_MODEL), _const_spec((1, D_MODEL)), _const_spec((D_MODEL, N_COLS)),
                _const_spec((1, ATTN_W)), _const_spec((1, ATTN_W)), _const_spec((BLOCK, BLOCK)),
                _const_spec((3, CONV_CH))]
    args = [x, n1, w_in, qn, kn, bd, conv_w]
    if sample:
        in_specs += [row(CONV_CH), row(CONV_CH)]
        args += list(states)
        out_specs = [row(ATTN_W), row(ATTN_W), row(ATTN_W), row(CONV_CH), row(D_MODEL), row(D_MODEL),
                     row(CONV_CH)]
        out_shape = [jax.ShapeDtypeStruct((n_tok, ATTN_W), F32)] * 3 + [
            jax.ShapeDtypeStruct((n_tok, CONV_CH), BF16),
            jax.ShapeDtypeStruct((n_tok, D_MODEL), BF16),
            jax.ShapeDtypeStruct((n_tok, D_MODEL), BF16),
            jax.ShapeDtypeStruct((n_tok, CONV_CH), F32)]
        scratch = []
    else:
        blocks_per_tile = tm // BLOCK
        out_specs = [row(ATTN_W), row(ATTN_W), row(ATTN_W), row(ATTN_W), row(ATTN_W),
                     pl.BlockSpec((1, blocks_per_tile, ATTN_W), lambda i: (i, 0, 0)),
                     row(CONV_CH), row(D_MODEL), row(D_MODEL),
                     pl.BlockSpec((1, 2, CONV_CH), lambda i: (i // tiles_per_seq, 0, 0))]
        out_shape = [jax.ShapeDtypeStruct((n_tok, ATTN_W), BF16),
                     jax.ShapeDtypeStruct((n_tok, ATTN_W), F32),
                     jax.ShapeDtypeStruct((n_tok, ATTN_W), F32),
                     jax.ShapeDtypeStruct((n_tok, ATTN_W), BF16),
                     jax.ShapeDtypeStruct((n_tok, ATTN_W), BF16),
                     jax.ShapeDtypeStruct((n_tiles, blocks_per_tile, ATTN_W), F32),
                     jax.ShapeDtypeStruct((n_tok, CONV_CH), BF16),
                     jax.ShapeDtypeStruct((n_tok, D_MODEL), BF16),
                     jax.ShapeDtypeStruct((n_tok, D_MODEL), BF16),
                     jax.ShapeDtypeStruct((n_tiles // tiles_per_seq, 2, CONV_CH), F32)]
        scratch = [pltpu.VMEM((8, CONV_CH), F32)]
    return pl.pallas_call(
        functools.partial(_inproj_kernel, tm=tm, sample=sample, tiles_per_seq=tiles_per_seq),
        grid=(n_tiles,),
        in_specs=in_specs,
        out_specs=out_specs,
        out_shape=out_shape,
        scratch_shapes=scratch,
        compiler_params=pltpu.CompilerParams(dimension_semantics=("arbitrary",),
                                             vmem_limit_bytes=VMEM_LIMIT),
    )(*args)


def _attn_kernel(q_ref, k_ref, v_ref, ksum_ref, bias_ref, o_ref, vt_ref, pen_ref, m_ref, acc_ref):
    qb = pl.program_id(2)
    n_blocks = k_ref.shape[1] // BLOCK

    @pl.when(qb == 0)
    def _():
        hrow = lax.broadcasted_iota(jnp.int32, (128, BLOCK), 0) // HEAD_DIM
        for kb in range(n_blocks):
            vt = v_ref[0, kb * BLOCK:(kb + 1) * BLOCK, :].astype(F32).T
            for e in range(2):
                vt_ref[e, kb] = jnp.where(hrow == e, vt, 1.0).astype(BF16)

    q = q_ref[0]
    lane_head = lax.broadcasted_iota(jnp.int32, q.shape, 1) // HEAD_DIM
    kmean = ksum_ref[0] * (1.0 / BLOCK)
    km_hi = kmean.astype(BF16)
    km_lo = (kmean - km_hi.astype(F32)).astype(BF16)
    km2 = jnp.concatenate([km_hi, km_lo], axis=0)
    blk = lax.broadcasted_iota(jnp.int32, (n_blocks, BLOCK), 0)
    key_i = lax.broadcasted_iota(jnp.int32, (BLOCK, BLOCK), 0)
    qry_i = lax.broadcasted_iota(jnp.int32, (BLOCK, BLOCK), 1)
    out_row_head = lax.broadcasted_iota(jnp.int32, (128, BLOCK), 0) // HEAD_DIM

    out_t = None
    for e in range(2):
        qe = jnp.where(lane_head == e, q, jnp.zeros_like(q))

        g2 = _dot_nt(km2, qe)
        g = g2[:n_blocks] + g2[n_blocks:]
        valid = blk < qb
        g = jnp.where(valid, g, -jnp.inf)
        rank = jnp.zeros(g.shape, jnp.int32)
        for j in range(n_blocks):
            gj = g[j:j + 1, :]
            beats = (gj > g) | ((gj == g) & (blk > j))
            rank = rank + beats.astype(jnp.int32)
        sel = valid & (rank < TOP_K)
        pen_ref[...] = jnp.where(sel, 0.0, NEG_BIG)

        k_own = k_ref[0, pl.ds(pl.multiple_of(qb * BLOCK, BLOCK), BLOCK), :]
        s = _dot_nt(k_own, qe) + bias_ref[e, 0]
        s = jnp.where(key_i <= qry_i, s, NEG_BIG)
        m = jnp.max(s, axis=0, keepdims=True)
        p = jnp.exp(s - m)
        m_ref[...] = m
        acc_ref[...] = _dot(vt_ref[e, qb], p.astype(BF16))

        def past_block(kb, with_bias):
            kblk = k_ref[0, pl.ds(pl.multiple_of(kb * BLOCK, BLOCK), BLOCK), :]
            s = _dot_nt(kblk, qe) + pen_ref[pl.ds(kb, 1), :]
            if with_bias:
                s = s + bias_ref[e, 1]
            m_old = m_ref[...]
            m_new = jnp.maximum(m_old, jnp.max(s, axis=0, keepdims=True))
            p = jnp.exp(s - m_new)
            acc_ref[...] = acc_ref[...] * jnp.exp(m_old - m_new) + _dot(vt_ref[e, kb], p.astype(BF16))
            m_ref[...] = m_new

        def far_body(kb, carry):
            past_block(kb, False)
            return carry

        lax.fori_loop(0, jnp.maximum(qb - 1, 0), far_body, 0)

        @pl.when(qb >= 1)
        def _():
            past_block(qb - 1, True)

        acc = acc_ref[...]
        other = (1 - e) * HEAD_DIM
        out_e = acc / acc[other:other + 1, :]
        out_t = out_e if e == 0 else jnp.where(out_row_head == 0, out_t, out_e)
    o_ref[0] = out_t.T.astype(o_ref.dtype)


def _prompt_attention(q, k, v, ksum, bias_tiles):
    b, s, _ = q.shape
    n_blocks = s // BLOCK
    return pl.pallas_call(
        _attn_kernel,
        grid=(b, N_HEADS // 2, n_blocks),
        in_specs=[
            pl.BlockSpec((1, BLOCK, 128), lambda bi, hp, qb: (bi, qb, hp)),
            pl.BlockSpec((1, s, 128), lambda bi, hp, qb: (bi, 0, hp)),
            pl.BlockSpec((1, s, 128), lambda bi, hp, qb: (bi, 0, hp)),
            pl.BlockSpec((1, n_blocks, 128), lambda bi, hp, qb: (bi, 0, hp)),
            pl.BlockSpec((2, 2, BLOCK, BLOCK), lambda bi, hp, qb: (hp, 0, 0, 0)),
        ],
        out_specs=pl.BlockSpec((1, BLOCK, 128), lambda bi, hp, qb: (bi, qb, hp)),
        out_shape=jax.ShapeDtypeStruct((b, s, ATTN_W), BF16),
        scratch_shapes=[
            pltpu.VMEM((2, n_blocks, 128, BLOCK), BF16),
            pltpu.VMEM((n_blocks, BLOCK), F32),
            pltpu.VMEM((1, BLOCK), F32),
            pltpu.VMEM((128, BLOCK), F32),
        ],
        compiler_params=pltpu.CompilerParams(
            dimension_semantics=("arbitrary", "arbitrary", "arbitrary"),
            vmem_limit_bytes=VMEM_LIMIT),
    )(q, k, v, ksum, bias_tiles)


def _merge_ffn_kernel(*refs, tm, sample, tiles_per_seq):
    if sample:
        (x_ref, at_ref, sin_ref, ga_ref, gb_ref, wa_ref, wc_ref, wo_ref, n2_ref, wup_ref, fcw_ref,
         wdn_ref, fs0_ref, fs1_ref, y_ref, g_ref, acc_ref) = refs
    else:
        (x_ref, at_ref, sin_ref, ga_ref, gb_ref, wa_ref, wc_ref, wo_ref, n2_ref, wup_ref, fcw_ref,
         wdn_ref, y_ref, fst_ref, acc_ref, gcar_ref) = refs

    a = _dot(at_ref[...].astype(BF16), wa_ref[...])
    b = _dot(sin_ref[...], wc_ref[...])
    merged = jax.nn.sigmoid(ga_ref[...].astype(F32)) * a + jax.nn.sigmoid(gb_ref[...].astype(F32)) * b
    x1 = x_ref[...] + _dot(merged.astype(BF16), wo_ref[...])
    ms = jnp.mean(x1 * x1, axis=-1, keepdims=True)
    xn2 = (x1 * lax.rsqrt(ms + EPS) * n2_ref[...]).astype(BF16)
    acc_ref[...] = x1

    if not sample:
        @pl.when(pl.program_id(0) % tiles_per_seq == 0)
        def _():
            gcar_ref[...] = jnp.zeros_like(gcar_ref)

    def chunk(c, carry):
        g = _dot(xn2, wup_ref[c])
        u = _dot(xn2, wup_ref[N_FF_CHUNKS + c])
        cw = fcw_ref[c]
        w0, w1, w2 = cw[0:1, :], cw[1:2, :], cw[2:3, :]
        if sample:
            gc = fs0_ref[c] * w0 + fs1_ref[c] * w1 + g * w2
            g_ref[c] = g
        else:
            ge = jnp.concatenate([gcar_ref[c], g], axis=0)
            gc = ge[6:6 + tm] * w0 + ge[7:7 + tm] * w1 + g * w2
            gcar_ref[c] = g[tm - 8:tm]
            fst_ref[0, c] = g[tm - 2:tm]
        hidden = (gc * jax.nn.sigmoid(gc) * u).astype(BF16)
        acc_ref[...] += _dot(hidden, wdn_ref[c])
        return carry

    lax.fori_loop(0, N_FF_CHUNKS, chunk, 0)
    y_ref[...] = acc_ref[...]


def _merge_ffn(x, attn, s_in, ga, gb, wa, wc, wo, n2, wup, fcw, wdn, states, *, sample, seq_len=1):
    n_tok = x.shape[0]
    tm = n_tok if sample else TOKEN_TILE
    n_tiles = n_tok // tm
    tiles_per_seq = 1 if sample else seq_len // tm

    def row(width):
        return pl.BlockSpec((tm, width), lambda i: (i, 0))

    in_specs = [row(D_MODEL), row(ATTN_W), row(CONV_CH), row(D_MODEL), row(D_MODEL),
                _const_spec((ATTN_W, D_MODEL)), _const_spec((CONV_CH, D_MODEL)),
                _const_spec((D_MODEL, D_MODEL)), _const_spec((1, D_MODEL)),
                _const_spec((2 * N_FF_CHUNKS, D_MODEL, FF_CHUNK)),
                _const_spec((N_FF_CHUNKS, 3, FF_CHUNK)),
                _const_spec((N_FF_CHUNKS, FF_CHUNK, D_MODEL))]
    args = [x, attn, s_in, ga, gb, wa, wc, wo, n2, wup, fcw, wdn]
    scratch = [pltpu.VMEM((tm, D_MODEL), F32)]
    if sample:
        in_specs += [_const_spec((N_FF_CHUNKS, tm, FF_CHUNK))] * 2
        args += list(states)
        out_specs = [row(D_MODEL), pl.BlockSpec((N_FF_CHUNKS, tm, FF_CHUNK), lambda i: (0, 0, 0))]
        out_shape = [jax.ShapeDtypeStruct((n_tok, D_MODEL), F32),
                     jax.ShapeDtypeStruct((N_FF_CHUNKS, tm, FF_CHUNK), F32)]
    else:
        n_seq = n_tiles // tiles_per_seq
        out_specs = [row(D_MODEL),
                     pl.BlockSpec((1, N_FF_CHUNKS, 2, FF_CHUNK), lambda i: (i // tiles_per_seq, 0, 0, 0))]
        out_shape = [jax.ShapeDtypeStruct((n_tok, D_MODEL), F32),
                     jax.ShapeDtypeStruct((n_seq, N_FF_CHUNKS, 2, FF_CHUNK), F32)]
        scratch.append(pltpu.VMEM((N_FF_CHUNKS, 8, FF_CHUNK), F32))
    return pl.pallas_call(
        functools.partial(_merge_ffn_kernel, tm=tm, sample=sample, tiles_per_seq=tiles_per_seq),
        grid=(n_tiles,),
        in_specs=in_specs,
        out_specs=out_specs,
        out_shape=out_shape,
        scratch_shapes=scratch,
        compiler_params=pltpu.CompilerParams(dimension_semantics=("arbitrary",),
                                             vmem_limit_bytes=VMEM_LIMIT),
    )(*args)


def _score_kernel(pt_ref, q_ref, *refs):
    page_refs = refs[:PAGES_PER_STEP]
    sc_ref = refs[PAGES_PER_STEP]
    q = q_ref[0]
    row = lax.broadcasted_iota(jnp.int32, (N_HEADS, ATTN_W), 0)
    lane_head = lax.broadcasted_iota(jnp.int32, (N_HEADS, ATTN_W), 1) // HEAD_DIM
    q_rows = jnp.where(row == lane_head, q, 0.0)
    q_hi = q_rows.astype(BF16).astype(F32)
    q2 = jnp.concatenate([q_hi, q_rows - q_hi], axis=0).astype(BF16)
    for i in range(PAGES_PER_STEP):
        page = page_refs[i][...].reshape(ATTN_W, PAGE).astype(BF16)
        s2 = _dot(q2, page)
        sc_ref[0, i] = s2[:N_HEADS] + s2[N_HEADS:]


def _cache_scores(cache_t, page_table_flat, q_rows, n_pages):
    n_seq = q_rows.shape[0]
    steps = n_pages // PAGES_PER_STEP

    def page_spec(i):
        return pl.BlockSpec(
            (None, None, N_HEADS, HEAD_DIM, PAGE),
            lambda b, g, pt: (0, pt[b * n_pages + g * PAGES_PER_STEP + i], 0, 0, 0))

    return pl.pallas_call(
        _score_kernel,
        grid_spec=pltpu.PrefetchScalarGridSpec(
            num_scalar_prefetch=1,
            grid=(n_seq, steps),
            in_specs=[pl.BlockSpec((1, 1, ATTN_W), lambda b, g, pt: (b, 0, 0))]
            + [page_spec(i) for i in range(PAGES_PER_STEP)],
            out_specs=pl.BlockSpec((1, PAGES_PER_STEP, N_HEADS, PAGE), lambda b, g, pt: (b, g, 0, 0)),
        ),
        out_shape=jax.ShapeDtypeStruct((n_seq, n_pages, N_HEADS, PAGE), F32),
        compiler_params=pltpu.CompilerParams(dimension_semantics=("arbitrary", "arbitrary"),
                                             vmem_limit_bytes=VMEM_LIMIT),
    )(page_table_flat, q_rows, *([cache_t] * PAGES_PER_STEP))


SEQ_PER_SELECT_STEP = 8


def _select_kernel(sc_ref, idx_ref):
    n_pages = sc_ref.shape[1]
    pages_per_block = BLOCK // PAGE
    n_blocks = n_pages // pages_per_block
    blk = lax.broadcasted_iota(jnp.int32, (n_blocks, N_HEADS, 128), 0)
    for s in range(SEQ_PER_SELECT_STEP):
        page_sum = jnp.sum(sc_ref[s], axis=-1, keepdims=True)
        block_sum = jnp.sum(page_sum.reshape(n_blocks, pages_per_block, N_HEADS, 1), axis=1)
        gate = jnp.broadcast_to(block_sum * (1.0 / BLOCK), (n_blocks, N_HEADS, 128))
        for r in range(TOP_K):
            best = jnp.max(gate, axis=0, keepdims=True)
            idx = jnp.min(jnp.where(gate == best, blk, n_blocks), axis=0, keepdims=True)
            idx_ref[s, r] = idx[0]
            gate = jnp.where(blk == idx, -jnp.inf, gate)


def _select_blocks(scores):
    n_seq, n_pages = scores.shape[:2]
    return pl.pallas_call(
        _select_kernel,
        grid=(n_seq // SEQ_PER_SELECT_STEP,),
        in_specs=[pl.BlockSpec((SEQ_PER_SELECT_STEP, n_pages, N_HEADS, PAGE), lambda i: (i, 0, 0, 0))],
        out_specs=pl.BlockSpec((SEQ_PER_SELECT_STEP, TOP_K, N_HEADS, 128), lambda i: (i, 0, 0, 0)),
        out_shape=jax.ShapeDtypeStruct((n_seq, TOP_K, N_HEADS, 128), jnp.int32),
    )(scores)


PAGES_PER_BLOCK = BLOCK // PAGE
CHUNKS_PER_HEAD = TOP_K * PAGES_PER_BLOCK
N_CHUNKS = N_HEADS * CHUNKS_PER_HEAD
KEYS_PER_HEAD = CHUNKS_PER_HEAD * PAGE


def _sample_attn_kernel(pt_ref, idx_ref, sc_ref, q_ref, kn_ref, vn_ref, brow_ref, ownb_ref, cv_ref,
                        o_ref, vbuf, logit_ref, sem, *, n_seq, n_pages):
    b = pl.program_id(0)
    slot = b % 2

    def picked_page(seq, h, r, j):
        n = idx_ref[(seq * TOP_K + r) * N_HEADS + h]
        return n, n * PAGES_PER_BLOCK + j

    def copies(seq, slot_):
        out = []
        for h in range(N_HEADS):
            for r in range(TOP_K):
                for j in range(PAGES_PER_BLOCK):
                    _, logical = picked_page(seq, h, r, j)
                    page = pt_ref[seq * n_pages + logical]
                    c = (h * TOP_K + r) * PAGES_PER_BLOCK + j
                    out.append(pltpu.make_async_copy(cv_ref.at[0, page, h], vbuf.at[slot_, c], sem.at[slot_]))
        return out

    @pl.when(b == 0)
    def _():
        for c in copies(0, 0):
            c.start()

    @pl.when(b + 1 < n_seq)
    def _():
        for c in copies(b + 1, 1 - slot):
            c.start()

    last_block = n_pages // PAGES_PER_BLOCK - 1
    for h in range(N_HEADS):
        for r in range(TOP_K):
            for j in range(PAGES_PER_BLOCK):
                n, logical = picked_page(b, h, r, j)
                piece = sc_ref[0, logical][h:h + 1, :]
                near = brow_ref[h][:, j * PAGE:(j + 1) * PAGE]
                piece = piece + jnp.where(n == last_block, near, jnp.zeros_like(near))
                c = r * PAGES_PER_BLOCK + j
                logit_ref[h:h + 1, c * PAGE:(c + 1) * PAGE] = piece
    logits = logit_ref[...]

    q = q_ref[0]
    own = jnp.sum(q * kn_ref[0], axis=-1, keepdims=True) + ownb_ref[:, 0, 0:1]
    m = jnp.maximum(jnp.max(logits, axis=-1, keepdims=True), own)
    p = jnp.exp(logits - m)
    p_own = jnp.exp(own - m)
    denom = jnp.sum(p, axis=-1, keepdims=True) + p_own

    p_wide = jnp.concatenate([p] * N_HEADS, axis=1)
    col = lax.broadcasted_iota(jnp.int32, p_wide.shape, 1)
    row = lax.broadcasted_iota(jnp.int32, p_wide.shape, 0)
    p_wide = jnp.where(col // KEYS_PER_HEAD == row, p_wide, 0.0)
    p16 = jnp.concatenate([p_wide, jnp.zeros_like(p_wide)], axis=0).astype(BF16)

    for c in copies(b, slot):
        c.wait()
    vt = jnp.concatenate([vbuf[slot, c] for c in range(N_CHUNKS)], axis=1).astype(BF16)
    ctx = _dot_nt(p16, vt)[:N_HEADS] + p_own * vn_ref[0]
    o_ref[0] = ctx / denom


def _sample_attention(page_table_flat, idx_flat, scores, q_heads, k_new, v_new, brow, ownb, cache_vt, n_pages):
    n_seq = q_heads.shape[0]
    head_spec = pl.BlockSpec((1, N_HEADS, HEAD_DIM), lambda b, pt, ix: (b, 0, 0))
    return pl.pallas_call(
        functools.partial(_sample_attn_kernel, n_seq=n_seq, n_pages=n_pages),
        grid_spec=pltpu.PrefetchScalarGridSpec(
            num_scalar_prefetch=2,
            grid=(n_seq,),
            in_specs=[pl.BlockSpec((1, n_pages, N_HEADS, PAGE), lambda b, pt, ix: (b, 0, 0, 0)),
                      head_spec, head_spec, head_spec,
                      pl.BlockSpec((N_HEADS, 1, BLOCK), lambda b, pt, ix: (0, 0, 0)),
                      pl.BlockSpec((N_HEADS, 1, 128), lambda b, pt, ix: (0, 0, 0)),
                      pl.BlockSpec(memory_space=pl.ANY)],
            out_specs=head_spec,
            scratch_shapes=[pltpu.VMEM((2, N_CHUNKS, HEAD_DIM, PAGE), F32),
                            pltpu.VMEM((N_HEADS, KEYS_PER_HEAD), F32),
                            pltpu.SemaphoreType.DMA((2,))],
        ),
        out_shape=jax.ShapeDtypeStruct((n_seq, N_HEADS, HEAD_DIM), F32),
        compiler_params=pltpu.CompilerParams(dimension_semantics=("arbitrary",),
                                             vmem_limit_bytes=VMEM_LIMIT),
    )(page_table_flat, idx_flat, scores, q_heads, k_new, v_new, brow, ownb, cache_vt)


def kernel(x_prompt, x_sample, cache_k, cache_v, page_table, state_conv, state_ffn, norm1_w, w_in, q_norm_w, k_norm_w, conv_w, w_attn_up, w_conv_out, w_o, norm2_w, w_ffn_up, ffn_conv_w, w_ffn_down, rel_bias):
    batch, seq, _ = x_prompt.shape
    n_seq = x_sample.shape[0]
    n_pages = page_table.shape[1]

    w_in_b = w_in[0].astype(BF16)
    wa = w_attn_up[0].astype(BF16)
    wc = w_conv_out[0].astype(BF16)
    wo = w_o[0].astype(BF16)
    wup = w_ffn_up[0].astype(BF16).reshape(D_MODEL, 2 * N_FF_CHUNKS, FF_CHUNK).transpose(1, 0, 2)
    wdn = w_ffn_down[0].astype(BF16).reshape(N_FF_CHUNKS, FF_CHUNK, D_MODEL)
    fcw = ffn_conv_w[0].reshape(3, N_FF_CHUNKS, FF_CHUNK).transpose(1, 0, 2)
    n1 = norm1_w[0].reshape(1, D_MODEL)
    n2 = norm2_w[0].reshape(1, D_MODEL)
    qn = jnp.tile(q_norm_w[0], N_HEADS).reshape(1, ATTN_W)
    kn = jnp.tile(k_norm_w[0], N_HEADS).reshape(1, ATTN_W)
    lane = np.arange(BLOCK)
    bd = jnp.asarray((lane[:, None] // HEAD_DIM == lane[None, :] // HEAD_DIM) / HEAD_DIM, BF16)
    cw = conv_w[0]

    bias_tiles, bias_row, bias_own = _bias_tables(rel_bias)

    xp = x_prompt.reshape(batch * seq, D_MODEL)
    (q_b, k_p, v_p, k_b, v_b, ksum_p, sin_p, ga_p, gb_p, conv_p) = _inproj(
        xp, n1, w_in_b, qn, kn, bd, cw, None, sample=False, seq_len=seq)
    attn_p = _prompt_attention(q_b.reshape(batch, seq, ATTN_W), k_b.reshape(batch, seq, ATTN_W),
                               v_b.reshape(batch, seq, ATTN_W),
                               ksum_p.reshape(batch, seq // BLOCK, ATTN_W), bias_tiles)
    y_p, ffn_p = _merge_ffn(xp, attn_p.reshape(batch * seq, ATTN_W), sin_p, ga_p, gb_p,
                            wa, wc, wo, n2, wup, fcw, wdn, None, sample=False, seq_len=seq)

    xs = x_sample.reshape(n_seq, D_MODEL)
    (q_s, k_s, v_s, sin_s, ga_s, gb_s, u_s) = _inproj(
        xs, n1, w_in_b, qn, kn, bd, cw, (state_conv[0, :, 0, :], state_conv[0, :, 1, :]), sample=True)
    pt_flat = page_table.reshape(-1)
    cache_kt = cache_k.transpose(0, 1, 3, 4, 2)
    cache_vt = cache_v.transpose(0, 1, 3, 4, 2)
    scores = _cache_scores(cache_kt, pt_flat, q_s.reshape(n_seq, 1, ATTN_W), n_pages)
    idx = _select_blocks(scores)[..., 0]
    attn_s = _sample_attention(pt_flat, idx.reshape(-1), scores, q_s.reshape(n_seq, N_HEADS, HEAD_DIM),
                               k_s.reshape(n_seq, N_HEADS, HEAD_DIM), v_s.reshape(n_seq, N_HEADS, HEAD_DIM),
                               bias_row, bias_own, cache_vt, n_pages)

    def ff_chunks(a):
        return a.reshape(n_seq, N_FF_CHUNKS, FF_CHUNK).transpose(1, 0, 2)

    y_s, g_s = _merge_ffn(xs, attn_s.reshape(n_seq, ATTN_W), sin_s, ga_s, gb_s,
                          wa, wc, wo, n2, wup, fcw, wdn,
                          (ff_chunks(state_ffn[0, :, 0, :]), ff_chunks(state_ffn[0, :, 1, :])), sample=True)
    g_s = g_s.transpose(1, 0, 2).reshape(n_seq, D_FF)

    return (
        y_p.reshape(batch, seq, D_MODEL),
        y_s.reshape(n_seq, 1, D_MODEL),
        k_p.reshape(1, batch, seq, N_HEADS, HEAD_DIM),
        v_p.reshape(1, batch, seq, N_HEADS, HEAD_DIM),
        conv_p[None],
        ffn_p.transpose(0, 2, 1, 3).reshape(1, batch, 2, D_FF),
        k_s.reshape(1, n_seq, 1, N_HEADS, HEAD_DIM),
        v_s.reshape(1, n_seq, 1, N_HEADS, HEAD_DIM),
        jnp.stack([state_conv[0, :, 1, :], u_s], axis=1)[None],
        jnp.stack([state_ffn[0, :, 1, :], g_s], axis=1)[None],
    )
```

```python
import functools
import math

import numpy as np
import jax
import jax.numpy as jnp
from jax import lax
from jax.experimental import pallas as pl
from jax.experimental.pallas import tpu as pltpu

F32 = jnp.float32
BF16 = jnp.bfloat16

D_MODEL = 1024
N_HEADS = 8
HEAD_DIM = 64
ATTN_W = N_HEADS * HEAD_DIM
CONV_CH = D_MODEL // 2
D_FF = 2816
BLOCK = 256
TOP_K = 3
PAGE = 128
NUM_BUCKETS = 32
MAX_DISTANCE = 128
EPS = 1e-6
N_COLS = 3 * ATTN_W + 3 * CONV_CH + 2 * D_MODEL

FF_CHUNK = 256
N_FF_CHUNKS = D_FF // FF_CHUNK
TOKEN_TILE = 512
PAGES_PER_STEP = 16
NEG_BIG = -1e30
VMEM_LIMIT = 60 * 1024 * 1024


def _bucket_thresholds():
    max_exact = NUM_BUCKETS // 2
    d = np.arange(0, 4 * MAX_DISTANCE)
    logd = np.log(np.maximum(d, 1) / max_exact)
    large = max_exact + (logd / math.log(MAX_DISTANCE / max_exact) * (NUM_BUCKETS - max_exact)).astype(np.int32)
    bucket = np.where(d < max_exact, d, np.minimum(large, NUM_BUCKETS - 1))
    return [int(np.argmax(bucket >= b)) for b in range(NUM_BUCKETS)]


BUCKET_THR = _bucket_thresholds()


def _dot(a, b):
    return jnp.dot(a, b, preferred_element_type=F32)


def _dot_nt(a, b):
    return lax.dot_general(a, b, (((1,), (1,)), ((), ())), preferred_element_type=F32)


def _const_spec(shape):
    nd = len(shape)
    return pl.BlockSpec(shape, lambda *_: (0,) * nd, pipeline_mode=pl.Buffered(1))


def _bias_from_distance(d, rb_ref, h, b_far):
    val = jnp.zeros(d.shape, F32)
    for b in range(NUM_BUCKETS - 2, -1, -1):
        val = jnp.where(d < BUCKET_THR[b + 1], rb_ref[b, h] - b_far, val)
    return val


def _bias_kernel(rb_ref, tile_ref, row_ref, own_ref):
    h = pl.program_id(0)
    b_far = rb_ref[NUM_BUCKETS - 1, h]
    j = lax.broadcasted_iota(jnp.int32, (BLOCK, BLOCK), 0)
    i = lax.broadcasted_iota(jnp.int32, (BLOCK, BLOCK), 1)
    tile_ref[0, 0] = jnp.where(j <= i, _bias_from_distance(i - j, rb_ref, h, b_far), NEG_BIG)
    tile_ref[0, 1] = _bias_from_distance(i - j + BLOCK, rb_ref, h, b_far)
    jr = lax.broadcasted_iota(jnp.int32, (1, BLOCK), 1)
    row_ref[0] = _bias_from_distance(BLOCK - jr, rb_ref, h, b_far)
    own_ref[0] = jnp.zeros((1, 128), F32) + (rb_ref[0, h] - b_far)


def _bias_tables(rel_bias):
    return pl.pallas_call(
        _bias_kernel,
        grid=(N_HEADS,),
        in_specs=[pl.BlockSpec(memory_space=pltpu.SMEM)],
        out_specs=[
            pl.BlockSpec((1, 2, BLOCK, BLOCK), lambda h: (h, 0, 0, 0)),
            pl.BlockSpec((1, 1, BLOCK), lambda h: (h, 0, 0)),
            pl.BlockSpec((1, 1, 128), lambda h: (h, 0, 0)),
        ],
        out_shape=[
            jax.ShapeDtypeStruct((N_HEADS, 2, BLOCK, BLOCK), F32),
            jax.ShapeDtypeStruct((N_HEADS, 1, BLOCK), F32),
            jax.ShapeDtypeStruct((N_HEADS, 1, 128), F32),
        ],
    )(rel_bias)


def _inproj_kernel(*refs, tm, sample, tiles_per_seq):
    if sample:
        (x_ref, n1_ref, w_ref, qn_ref, kn_ref, bd_ref, cw_ref, st0_ref, st1_ref,
         q_ref, k_ref, v_ref, sin_ref, ga_ref, gb_ref, u_ref) = refs
    else:
        (x_ref, n1_ref, w_ref, qn_ref, kn_ref, bd_ref, cw_ref,
         q_ref, kt_ref, vt_ref, kb_ref, vtb_ref, ksum_ref, sin_ref, ga_ref, gb_ref, cst_ref,
         ucar_ref) = refs

    x = x_ref[...]
    ms = jnp.mean(x * x, axis=-1, keepdims=True)
    xn = (x * lax.rsqrt(ms + EPS) * n1_ref[...]).astype(BF16)

    def proj(a, b):
        return _dot(xn, w_ref[:, a:b])

    def head_norm(t, w_row):
        sq = (t * t).astype(BF16)
        bd = bd_ref[...]
        half = ATTN_W // 2
        msq = jnp.concatenate([_dot(sq[:, :half], bd), _dot(sq[:, half:], bd)], axis=1)
        return t * lax.rsqrt(msq + EPS) * w_row

    q = head_norm(proj(0, ATTN_W), qn_ref[...]) * (HEAD_DIM ** -0.5)
    k = head_norm(proj(ATTN_W, 2 * ATTN_W), kn_ref[...])
    v = proj(2 * ATTN_W, 3 * ATTN_W)
    c0 = 3 * ATTN_W
    cb = proj(c0, c0 + CONV_CH)
    u = proj(c0 + CONV_CH, c0 + 2 * CONV_CH) * proj(c0 + 2 * CONV_CH, c0 + 3 * CONV_CH)
    g0 = c0 + 3 * CONV_CH
    ga_ref[...] = proj(g0, g0 + D_MODEL).astype(ga_ref.dtype)
    gb_ref[...] = proj(g0 + D_MODEL, g0 + 2 * D_MODEL).astype(gb_ref.dtype)

    w0, w1, w2 = cw_ref[0:1, :], cw_ref[1:2, :], cw_ref[2:3, :]
    if sample:
        q_ref[...] = q
        k_ref[...] = k
        v_ref[...] = v
        uc = st0_ref[...] * w0 + st1_ref[...] * w1 + u * w2
        u_ref[...] = u
    else:
        q_ref[...] = q.astype(BF16)
        kb_ref[...] = k.astype(BF16)
        kt_ref[0] = k.T
        v_t = v.T
        vt_ref[0] = v_t
        for r in range(tm // BLOCK):
            vtb_ref[0, r] = v_t[:, r * BLOCK:(r + 1) * BLOCK].astype(BF16)
            ksum_ref[0, r:r + 1, :] = jnp.sum(k[r * BLOCK:(r + 1) * BLOCK], axis=0, keepdims=True)

        @pl.when(pl.program_id(0) % tiles_per_seq == 0)
        def _():
            ucar_ref[...] = jnp.zeros_like(ucar_ref)

        ue = jnp.concatenate([ucar_ref[...], u], axis=0)
        uc = ue[6:6 + tm] * w0 + ue[7:7 + tm] * w1 + u * w2
        ucar_ref[...] = u[tm - 8:tm]
        cst_ref[0] = u[tm - 2:tm]
    sin_ref[...] = (cb * uc).astype(sin_ref.dtype)


def _inproj(x, n1, w_in, qn, kn, bd, conv_w, states, *, sample, seq_len=1):
    n_tok = x.shape[0]
    tm = n_tok if sample else TOKEN_TILE
    n_tiles = n_tok // tm
    tiles_per_seq = 1 if sample else seq_len // tm

    def row(width):
        return pl.BlockSpec((tm, width), lambda i: (i, 0))

    in_specs = [row(D_MODEL), _const_spec((1, D_MODEL)), _const_spec((D_MODEL, N_COLS)),
                _const_spec((1, ATTN_W)), _const_spec((1, ATTN_W)), _const_spec((BLOCK, BLOCK)),
                _const_spec((3, CONV_CH))]
    args = [x, n1, w_in, qn, kn, bd, conv_w]
    if sample:
        in_specs += [row(CONV_CH), row(CONV_CH)]
        args += list(states)
        out_specs = [row(ATTN_W), row(ATTN_W), row(ATTN_W), row(CONV_CH), row(D_MODEL), row(D_MODEL),
                     row(CONV_CH)]
        out_shape = [jax.ShapeDtypeStruct((n_tok, ATTN_W), F32)] * 3 + [
            jax.ShapeDtypeStruct((n_tok, CONV_CH), BF16),
            jax.ShapeDtypeStruct((n_tok, D_MODEL), BF16),
            jax.ShapeDtypeStruct((n_tok, D_MODEL), BF16),
            jax.ShapeDtypeStruct((n_tok, CONV_CH), F32)]
        scratch = []
    else:
        blocks_per_tile = tm // BLOCK
        n_seq = n_tiles // tiles_per_seq
        t_spec = pl.BlockSpec((1, ATTN_W, tm), lambda i: (i // tiles_per_seq, 0, i % tiles_per_seq))
        out_specs = [row(ATTN_W), t_spec, t_spec, row(ATTN_W),
                     pl.BlockSpec((1, blocks_per_tile, ATTN_W, BLOCK),
                                  lambda i: (i // tiles_per_seq, i % tiles_per_seq, 0, 0)),
                     pl.BlockSpec((1, blocks_per_tile, ATTN_W), lambda i: (i, 0, 0)),
                     row(CONV_CH), row(D_MODEL), row(D_MODEL),
                     pl.BlockSpec((1, 2, CONV_CH), lambda i: (i // tiles_per_seq, 0, 0))]
        out_shape = [jax.ShapeDtypeStruct((n_tok, ATTN_W), BF16),
                     jax.ShapeDtypeStruct((n_seq, ATTN_W, seq_len), F32),
                     jax.ShapeDtypeStruct((n_seq, ATTN_W, seq_len), F32),
                     jax.ShapeDtypeStruct((n_tok, ATTN_W), BF16),
                     jax.ShapeDtypeStruct((n_seq, seq_len // BLOCK, ATTN_W, BLOCK), BF16),
                     jax.ShapeDtypeStruct((n_tiles, blocks_per_tile, ATTN_W), F32),
                     jax.ShapeDtypeStruct((n_tok, CONV_CH), BF16),
                     jax.ShapeDtypeStruct((n_tok, D_MODEL), BF16),
                     jax.ShapeDtypeStruct((n_tok, D_MODEL), BF16),
                     jax.ShapeDtypeStruct((n_tiles // tiles_per_seq, 2, CONV_CH), F32)]
        scratch = [pltpu.VMEM((8, CONV_CH), F32)]
    return pl.pallas_call(
        functools.partial(_inproj_kernel, tm=tm, sample=sample, tiles_per_seq=tiles_per_seq),
        grid=(n_tiles,),
        in_specs=in_specs,
        out_specs=out_specs,
        out_shape=out_shape,
        scratch_shapes=scratch,
        compiler_params=pltpu.CompilerParams(dimension_semantics=("arbitrary",),
                                             vmem_limit_bytes=VMEM_LIMIT),
    )(*args)


def _attn_kernel(q_ref, k_ref, vt_ref, ksum_ref, bias_ref, o_ref, pen_ref, m_ref, l_ref, acc_ref):
    qb = pl.program_id(2)
    n_blocks = k_ref.shape[1] // BLOCK
    nq = 2 * BLOCK

    q = q_ref[0]
    lane_head = lax.broadcasted_iota(jnp.int32, q.shape, 1) // HEAD_DIM
    zero = jnp.zeros_like(q)
    q2 = jnp.concatenate([jnp.where(lane_head == 0, q, zero), jnp.where(lane_head == 1, q, zero)], axis=0)

    kmean = ksum_ref[0] * (1.0 / BLOCK)
    km_hi = kmean.astype(BF16)
    km_lo = (kmean - km_hi.astype(F32)).astype(BF16)
    g2 = _dot_nt(jnp.concatenate([km_hi, km_lo], axis=0), q2)
    g = g2[:n_blocks] + g2[n_blocks:]
    blk = lax.broadcasted_iota(jnp.int32, (n_blocks, nq), 0)
    valid = blk < qb
    g = jnp.where(valid, g, -jnp.inf)
    rank = jnp.zeros(g.shape, jnp.int32)
    for j in range(n_blocks):
        gj = g[j:j + 1, :]
        beats = (gj > g) | ((gj == g) & (blk > j))
        rank = rank + beats.astype(jnp.int32)
    sel = valid & (rank < TOP_K)
    pen_ref[...] = jnp.where(sel, 0.0, NEG_BIG)

    def scores(kb):
        kblk = k_ref[0, pl.ds(pl.multiple_of(kb * BLOCK, BLOCK), BLOCK), :]
        return _dot_nt(kblk, q2)

    def penalty(kb):
        return pen_ref[pl.ds(kb, 1), :]

    def pv(kb, p):
        vt = vt_ref[0, kb]
        return jnp.concatenate([_dot(vt[:HEAD_DIM], p[:, :BLOCK]), _dot(vt[HEAD_DIM:], p[:, BLOCK:])], axis=1)

    def bias_tile(t):
        return jnp.concatenate([bias_ref[0, t], bias_ref[1, t]], axis=1)

    s = scores(qb) + bias_tile(0)
    m = jnp.max(s, axis=0, keepdims=True)
    p = jnp.exp(s - m)
    m_ref[...] = m
    l_ref[...] = jnp.sum(p, axis=0, keepdims=True)
    acc_ref[...] = pv(qb, p.astype(BF16))

    def update(blocks):
        m_old = m_ref[...]
        m_new = m_old
        for _, s_ in blocks:
            m_new = jnp.maximum(m_new, jnp.max(s_, axis=0, keepdims=True))
        alpha = jnp.exp(m_old - m_new)
        l_new = alpha * l_ref[...]
        acc_new = alpha * acc_ref[...]
        for kb_, s_ in blocks:
            p_ = jnp.exp(s_ - m_new)
            l_new = l_new + jnp.sum(p_, axis=0, keepdims=True)
            acc_new = acc_new + pv(kb_, p_.astype(BF16))
        m_ref[...] = m_new
        l_ref[...] = l_new
        acc_ref[...] = acc_new

    n_far = jnp.maximum(qb - 1, 0)

    def far_pair(i, carry):
        kb = 2 * i
        update([(kb, scores(kb) + penalty(kb)), (kb + 1, scores(kb + 1) + penalty(kb + 1))])
        return carry

    lax.fori_loop(0, n_far // 2, far_pair, 0)

    def prev_scores():
        return scores(qb - 1) + penalty(qb - 1) + bias_tile(1)

    @pl.when((qb >= 1) & (n_far % 2 == 0))
    def _():
        update([(qb - 1, prev_scores())])

    @pl.when((qb >= 1) & (n_far % 2 == 1))
    def _():
        kb = n_far - 1
        update([(kb, scores(kb) + penalty(kb)), (qb - 1, prev_scores())])

    out = acc_ref[...] / l_ref[...]
    out_t = jnp.concatenate([out[:, :BLOCK], out[:, BLOCK:]], axis=0)
    o_ref[0] = out_t.T.astype(o_ref.dtype)


def _prompt_attention(q, k, vt_blocks, ksum, bias_tiles):
    b, s, _ = q.shape
    n_blocks = s // BLOCK
    return pl.pallas_call(
        _attn_kernel,
        grid=(b, N_HEADS // 2, n_blocks),
        in_specs=[
            pl.BlockSpec((1, BLOCK, 128), lambda bi, hp, qb: (bi, qb, hp)),
            pl.BlockSpec((1, s, 128), lambda bi, hp, qb: (bi, 0, hp)),
            pl.BlockSpec((1, n_blocks, 128, BLOCK), lambda bi, hp, qb: (bi, 0, hp, 0)),
            pl.BlockSpec((1, n_blocks, 128), lambda bi, hp, qb: (bi, 0, hp)),
            pl.BlockSpec((2, 2, BLOCK, BLOCK), lambda bi, hp, qb: (hp, 0, 0, 0)),
        ],
        out_specs=pl.BlockSpec((1, BLOCK, 128), lambda bi, hp, qb: (bi, qb, hp)),
        out_shape=jax.ShapeDtypeStruct((b, s, ATTN_W), BF16),
        scratch_shapes=[
            pltpu.VMEM((n_blocks, 2 * BLOCK), F32),
            pltpu.VMEM((1, 2 * BLOCK), F32),
            pltpu.VMEM((1, 2 * BLOCK), F32),
            pltpu.VMEM((HEAD_DIM, 2 * BLOCK), F32),
        ],
        compiler_params=pltpu.CompilerParams(
            dimension_semantics=("arbitrary", "arbitrary", "arbitrary"),
            vmem_limit_bytes=VMEM_LIMIT),
    )(q, k, vt_blocks, ksum, bias_tiles)


def _merge_ffn_kernel(*refs, tm, sample, tiles_per_seq):
    if sample:
        (x_ref, at_ref, sin_ref, ga_ref, gb_ref, wa_ref, wc_ref, wo_ref, n2_ref, wup_ref, fcw_ref,
         wdn_ref, fs0_ref, fs1_ref, y_ref, g_ref, acc_ref) = refs
    else:
        (x_ref, at_ref, sin_ref, ga_ref, gb_ref, wa_ref, wc_ref, wo_ref, n2_ref, wup_ref, fcw_ref,
         wdn_ref, y_ref, fst_ref, acc_ref, gcar_ref) = refs

    a = _dot(at_ref[...].astype(BF16), wa_ref[...])
    b = _dot(sin_ref[...], wc_ref[...])
    merged = jax.nn.sigmoid(ga_ref[...].astype(F32)) * a + jax.nn.sigmoid(gb_ref[...].astype(F32)) * b
    x1 = x_ref[...] + _dot(merged.astype(BF16), wo_ref[...])
    ms = jnp.mean(x1 * x1, axis=-1, keepdims=True)
    xn2 = (x1 * lax.rsqrt(ms + EPS) * n2_ref[...]).astype(BF16)
    acc_ref[...] = x1

    if not sample:
        @pl.when(pl.program_id(0) % tiles_per_seq == 0)
        def _():
            gcar_ref[...] = jnp.zeros_like(gcar_ref)

    def chunk(c, carry):
        g = _dot(xn2, wup_ref[c])
        u = _dot(xn2, wup_ref[N_FF_CHUNKS + c])
        cw = fcw_ref[c]
        w0, w1, w2 = cw[0:1, :], cw[1:2, :], cw[2:3, :]
        if sample:
            gc = fs0_ref[c] * w0 + fs1_ref[c] * w1 + g * w2
            g_ref[c] = g
        else:
            ge = jnp.concatenate([gcar_ref[c], g], axis=0)
            gc = ge[6:6 + tm] * w0 + ge[7:7 + tm] * w1 + g * w2
            gcar_ref[c] = g[tm - 8:tm]
            fst_ref[0, c] = g[tm - 2:tm]
        hidden = (gc * jax.nn.sigmoid(gc) * u).astype(BF16)
        acc_ref[...] += _dot(hidden, wdn_ref[c])
        return carry

    for c in range(N_FF_CHUNKS):
        chunk(c, 0)
    y_ref[...] = acc_ref[...]


def _merge_ffn(x, attn, s_in, ga, gb, wa, wc, wo, n2, wup, fcw, wdn, states, *, sample, seq_len=1):
    n_tok = x.shape[0]
    tm = n_tok if sample else TOKEN_TILE
    n_tiles = n_tok // tm
    tiles_per_seq = 1 if sample else seq_len // tm

    def row(width):
        return pl.BlockSpec((tm, width), lambda i: (i, 0))

    in_specs = [row(D_MODEL), row(ATTN_W), row(CONV_CH), row(D_MODEL), row(D_MODEL),
                _const_spec((ATTN_W, D_MODEL)), _const_spec((CONV_CH, D_MODEL)),
                _const_spec((D_MODEL, D_MODEL)), _const_spec((1, D_MODEL)),
                _const_spec((2 * N_FF_CHUNKS, D_MODEL, FF_CHUNK)),
                _const_spec((N_FF_CHUNKS, 3, FF_CHUNK)),
                _const_spec((N_FF_CHUNKS, FF_CHUNK, D_MODEL))]
    args = [x, attn, s_in, ga, gb, wa, wc, wo, n2, wup, fcw, wdn]
    scratch = [pltpu.VMEM((tm, D_MODEL), F32)]
    if sample:
        in_specs += [_const_spec((N_FF_CHUNKS, tm, FF_CHUNK))] * 2
        args += list(states)
        out_specs = [row(D_MODEL), pl.BlockSpec((N_FF_CHUNKS, tm, FF_CHUNK), lambda i: (0, 0, 0))]
        out_shape = [jax.ShapeDtypeStruct((n_tok, D_MODEL), F32),
                     jax.ShapeDtypeStruct((N_FF_CHUNKS, tm, FF_CHUNK), F32)]
    else:
        n_seq = n_tiles // tiles_per_seq
        out_specs = [row(D_MODEL),
                     pl.BlockSpec((1, N_FF_CHUNKS, 2, FF_CHUNK), lambda i: (i // tiles_per_seq, 0, 0, 0))]
        out_shape = [jax.ShapeDtypeStruct((n_tok, D_MODEL), F32),
                     jax.ShapeDtypeStruct((n_seq, N_FF_CHUNKS, 2, FF_CHUNK), F32)]
        scratch.append(pltpu.VMEM((N_FF_CHUNKS, 8, FF_CHUNK), F32))
    return pl.pallas_call(
        functools.partial(_merge_ffn_kernel, tm=tm, sample=sample, tiles_per_seq=tiles_per_seq),
        grid=(n_tiles,),
        in_specs=in_specs,
        out_specs=out_specs,
        out_shape=out_shape,
        scratch_shapes=scratch,
        compiler_params=pltpu.CompilerParams(dimension_semantics=("arbitrary",),
                                             vmem_limit_bytes=VMEM_LIMIT),
    )(*args)


def _score_kernel(pt_ref, q_ref, *refs):
    page_refs = refs[:PAGES_PER_STEP]
    sc_ref = refs[PAGES_PER_STEP]
    q = q_ref[0]
    row = lax.broadcasted_iota(jnp.int32, (N_HEADS, ATTN_W), 0)
    lane_head = lax.broadcasted_iota(jnp.int32, (N_HEADS, ATTN_W), 1) // HEAD_DIM
    q_rows = jnp.where(row == lane_head, q, 0.0)
    q_hi = q_rows.astype(BF16).astype(F32)
    q2 = jnp.concatenate([q_hi, q_rows - q_hi], axis=0).astype(BF16)
    for i in range(PAGES_PER_STEP):
        page = page_refs[i][...].reshape(ATTN_W, PAGE).astype(BF16)
        s2 = _dot(q2, page)
        sc_ref[0, i] = s2[:N_HEADS] + s2[N_HEADS:]


def _cache_scores(cache_t, page_table_flat, q_rows, n_pages):
    n_seq = q_rows.shape[0]
    steps = n_pages // PAGES_PER_STEP

    def page_spec(i):
        return pl.BlockSpec(
            (None, None, N_HEADS, HEAD_DIM, PAGE),
            lambda b, g, pt: (0, pt[b * n_pages + g * PAGES_PER_STEP + i], 0, 0, 0))

    return pl.pallas_call(
        _score_kernel,
        grid_spec=pltpu.PrefetchScalarGridSpec(
            num_scalar_prefetch=1,
            grid=(n_seq, steps),
            in_specs=[pl.BlockSpec((1, 1, ATTN_W), lambda b, g, pt: (b, 0, 0))]
            + [page_spec(i) for i in range(PAGES_PER_STEP)],
            out_specs=pl.BlockSpec((1, PAGES_PER_STEP, N_HEADS, PAGE), lambda b, g, pt: (b, g, 0, 0)),
        ),
        out_shape=jax.ShapeDtypeStruct((n_seq, n_pages, N_HEADS, PAGE), F32),
        compiler_params=pltpu.CompilerParams(dimension_semantics=("arbitrary", "arbitrary"),
                                             vmem_limit_bytes=VMEM_LIMIT),
    )(page_table_flat, q_rows, *([cache_t] * PAGES_PER_STEP))


SEQ_PER_SELECT_STEP = 8


def _select_kernel(sc_ref, idx_ref):
    n_pages = sc_ref.shape[1]
    pages_per_block = BLOCK // PAGE
    n_blocks = n_pages // pages_per_block
    blk = lax.broadcasted_iota(jnp.int32, (n_blocks, N_HEADS, 128), 0)
    for s in range(SEQ_PER_SELECT_STEP):
        page_sum = jnp.sum(sc_ref[s], axis=-1, keepdims=True)
        block_sum = jnp.sum(page_sum.reshape(n_blocks, pages_per_block, N_HEADS, 1), axis=1)
        gate = jnp.broadcast_to(block_sum * (1.0 / BLOCK), (n_blocks, N_HEADS, 128))
        for r in range(TOP_K):
            best = jnp.max(gate, axis=0, keepdims=True)
            idx = jnp.min(jnp.where(gate == best, blk, n_blocks), axis=0, keepdims=True)
            idx_ref[s, r] = idx[0]
            gate = jnp.where(blk == idx, -jnp.inf, gate)


def _select_blocks(scores):
    n_seq, n_pages = scores.shape[:2]
    return pl.pallas_call(
        _select_kernel,
        grid=(n_seq // SEQ_PER_SELECT_STEP,),
        in_specs=[pl.BlockSpec((SEQ_PER_SELECT_STEP, n_pages, N_HEADS, PAGE), lambda i: (i, 0, 0, 0))],
        out_specs=pl.BlockSpec((SEQ_PER_SELECT_STEP, TOP_K, N_HEADS, 128), lambda i: (i, 0, 0, 0)),
        out_shape=jax.ShapeDtypeStruct((n_seq, TOP_K, N_HEADS, 128), jnp.int32),
    )(scores)


PAGES_PER_BLOCK = BLOCK // PAGE
CHUNKS_PER_HEAD = TOP_K * PAGES_PER_BLOCK
N_CHUNKS = N_HEADS * CHUNKS_PER_HEAD
KEYS_PER_HEAD = CHUNKS_PER_HEAD * PAGE


def _sample_attn_kernel(pt_ref, idx_ref, sc_ref, q_ref, kn_ref, vn_ref, brow_ref, ownb_ref, cv_ref,
                        o_ref, vbuf, logit_ref, sem, *, n_seq, n_pages):
    b = pl.program_id(0)
    slot = b % 2

    def picked_page(seq, h, r, j):
        n = idx_ref[(seq * TOP_K + r) * N_HEADS + h]
        return n, n * PAGES_PER_BLOCK + j

    def copies(seq, slot_):
        out = []
        for h in range(N_HEADS):
            for r in range(TOP_K):
                for j in range(PAGES_PER_BLOCK):
                    _, logical = picked_page(seq, h, r, j)
                    page = pt_ref[seq * n_pages + logical]
                    c = (h * TOP_K + r) * PAGES_PER_BLOCK + j
                    out.append(pltpu.make_async_copy(cv_ref.at[0, page, h], vbuf.at[slot_, c], sem.at[slot_]))
        return out

    @pl.when(b == 0)
    def _():
        for c in copies(0, 0):
            c.start()

    @pl.when(b + 1 < n_seq)
    def _():
        for c in copies(b + 1, 1 - slot):
            c.start()

    last_block = n_pages // PAGES_PER_BLOCK - 1
    for h in range(N_HEADS):
        for r in range(TOP_K):
            for j in range(PAGES_PER_BLOCK):
                n, logical = picked_page(b, h, r, j)
                piece = sc_ref[0, logical][h:h + 1, :]
                near = brow_ref[h][:, j * PAGE:(j + 1) * PAGE]
                piece = piece + jnp.where(n == last_block, near, jnp.zeros_like(near))
                c = r * PAGES_PER_BLOCK + j
                logit_ref[h:h + 1, c * PAGE:(c + 1) * PAGE] = piece
    logits = logit_ref[...]

    q = q_ref[0]
    own = jnp.sum(q * kn_ref[0], axis=-1, keepdims=True) + ownb_ref[:, 0, 0:1]
    m = jnp.maximum(jnp.max(logits, axis=-1, keepdims=True), own)
    p = jnp.exp(logits - m)
    p_own = jnp.exp(own - m)
    denom = jnp.sum(p, axis=-1, keepdims=True) + p_own

    p_wide = jnp.concatenate([p] * N_HEADS, axis=1)
    col = lax.broadcasted_iota(jnp.int32, p_wide.shape, 1)
    row = lax.broadcasted_iota(jnp.int32, p_wide.shape, 0)
    p_wide = jnp.where(col // KEYS_PER_HEAD == row, p_wide, 0.0)
    p16 = jnp.concatenate([p_wide, jnp.zeros_like(p_wide)], axis=0).astype(BF16)

    for c in copies(b, slot):
        c.wait()
    vt = jnp.concatenate([vbuf[slot, c] for c in range(N_CHUNKS)], axis=1).astype(BF16)
    ctx = _dot_nt(p16, vt)[:N_HEADS] + p_own * vn_ref[0]
    o_ref[0] = ctx / denom


def _sample_attention(page_table_flat, idx_flat, scores, q_heads, k_new, v_new, brow, ownb, cache_vt, n_pages):
    n_seq = q_heads.shape[0]
    head_spec = pl.BlockSpec((1, N_HEADS, HEAD_DIM), lambda b, pt, ix: (b, 0, 0))
    return pl.pallas_call(
        functools.partial(_sample_attn_kernel, n_seq=n_seq, n_pages=n_pages),
        grid_spec=pltpu.PrefetchScalarGridSpec(
            num_scalar_prefetch=2,
            grid=(n_seq,),
            in_specs=[pl.BlockSpec((1, n_pages, N_HEADS, PAGE), lambda b, pt, ix: (b, 0, 0, 0)),
                      head_spec, head_spec, head_spec,
                      pl.BlockSpec((N_HEADS, 1, BLOCK), lambda b, pt, ix: (0, 0, 0)),
                      pl.BlockSpec((N_HEADS, 1, 128), lambda b, pt, ix: (0, 0, 0)),
                      pl.BlockSpec(memory_space=pl.ANY)],
            out_specs=head_spec,
            scratch_shapes=[pltpu.VMEM((2, N_CHUNKS, HEAD_DIM, PAGE), F32),
                            pltpu.VMEM((N_HEADS, KEYS_PER_HEAD), F32),
                            pltpu.SemaphoreType.DMA((2,))],
        ),
        out_shape=jax.ShapeDtypeStruct((n_seq, N_HEADS, HEAD_DIM), F32),
        compiler_params=pltpu.CompilerParams(dimension_semantics=("arbitrary",),
                                             vmem_limit_bytes=VMEM_LIMIT),
    )(page_table_flat, idx_flat, scores, q_heads, k_new, v_new, brow, ownb, cache_vt)


def kernel(x_prompt, x_sample, cache_k, cache_v, page_table, state_conv, state_ffn, norm1_w, w_in, q_norm_w, k_norm_w, conv_w, w_attn_up, w_conv_out, w_o, norm2_w, w_ffn_up, ffn_conv_w, w_ffn_down, rel_bias):
    batch, seq, _ = x_prompt.shape
    n_seq = x_sample.shape[0]
    n_pages = page_table.shape[1]

    w_in_b = w_in[0].astype(BF16)
    wa = w_attn_up[0].astype(BF16)
    wc = w_conv_out[0].astype(BF16)
    wo = w_o[0].astype(BF16)
    wup = w_ffn_up[0].astype(BF16).reshape(D_MODEL, 2 * N_FF_CHUNKS, FF_CHUNK).transpose(1, 0, 2)
    wdn = w_ffn_down[0].astype(BF16).reshape(N_FF_CHUNKS, FF_CHUNK, D_MODEL)
    fcw = ffn_conv_w[0].reshape(3, N_FF_CHUNKS, FF_CHUNK).transpose(1, 0, 2)
    n1 = norm1_w[0].reshape(1, D_MODEL)
    n2 = norm2_w[0].reshape(1, D_MODEL)
    qn = jnp.tile(q_norm_w[0], N_HEADS).reshape(1, ATTN_W)
    kn = jnp.tile(k_norm_w[0], N_HEADS).reshape(1, ATTN_W)
    lane = np.arange(BLOCK)
    bd = jnp.asarray((lane[:, None] // HEAD_DIM == lane[None, :] // HEAD_DIM) / HEAD_DIM, BF16)
    cw = conv_w[0]

    bias_tiles, bias_row, bias_own = _bias_tables(rel_bias)

    xp = x_prompt.reshape(batch * seq, D_MODEL)
    (q_b, kt_p, vt_p, k_b, vt_b, ksum_p, sin_p, ga_p, gb_p, conv_p) = _inproj(
        xp, n1, w_in_b, qn, kn, bd, cw, None, sample=False, seq_len=seq)
    attn_p = _prompt_attention(q_b.reshape(batch, seq, ATTN_W), k_b.reshape(batch, seq, ATTN_W), vt_b,
                               ksum_p.reshape(batch, seq // BLOCK, ATTN_W), bias_tiles)
    y_p, ffn_p = _merge_ffn(xp, attn_p.reshape(batch * seq, ATTN_W), sin_p, ga_p, gb_p,
                            wa, wc, wo, n2, wup, fcw, wdn, None, sample=False, seq_len=seq)

    xs = x_sample.reshape(n_seq, D_MODEL)
    (q_s, k_s, v_s, sin_s, ga_s, gb_s, u_s) = _inproj(
        xs, n1, w_in_b, qn, kn, bd, cw, (state_conv[0, :, 0, :], state_conv[0, :, 1, :]), sample=True)
    pt_flat = page_table.reshape(-1)
    cache_kt = cache_k.transpose(0, 1, 3, 4, 2)
    cache_vt = cache_v.transpose(0, 1, 3, 4, 2)
    scores = _cache_scores(cache_kt, pt_flat, q_s.reshape(n_seq, 1, ATTN_W), n_pages)
    idx = _select_blocks(scores)[..., 0]
    attn_s = _sample_attention(pt_flat, idx.reshape(-1), scores, q_s.reshape(n_seq, N_HEADS, HEAD_DIM),
                               k_s.reshape(n_seq, N_HEADS, HEAD_DIM), v_s.reshape(n_seq, N_HEADS, HEAD_DIM),
                               bias_row, bias_own, cache_vt, n_pages)

    def ff_chunks(a):
        return a.reshape(n_seq, N_FF_CHUNKS, FF_CHUNK).transpose(1, 0, 2)

    y_s, g_s = _merge_ffn(xs, attn_s.reshape(n_seq, ATTN_W), sin_s, ga_s, gb_s,
                          wa, wc, wo, n2, wup, fcw, wdn,
                          (ff_chunks(state_ffn[0, :, 0, :]), ff_chunks(state_ffn[0, :, 1, :])), sample=True)
    g_s = g_s.transpose(1, 0, 2).reshape(n_seq, D_FF)

    return (
        y_p.reshape(batch, seq, D_MODEL),
        y_s.reshape(n_seq, 1, D_MODEL),
        kt_p.reshape(1, batch, N_HEADS, HEAD_DIM, seq).transpose(0, 1, 4, 2, 3),
        vt_p.reshape(1, batch, N_HEADS, HEAD_DIM, seq).transpose(0, 1, 4, 2, 3),
        conv_p[None],
        ffn_p.transpose(0, 2, 1, 3).reshape(1, batch, 2, D_FF),
        k_s.reshape(1, n_seq, 1, N_HEADS, HEAD_DIM),
        v_s.reshape(1, n_seq, 1, N_HEADS, HEAD_DIM),
        jnp.stack([state_conv[0, :, 1, :], u_s], axis=1)[None],
        jnp.stack([state_ffn[0, :, 1, :], g_s], axis=1)[None],
    )
```

```python
import functools
import math

import numpy as np
import jax
import jax.numpy as jnp
from jax import lax
from jax.experimental import pallas as pl
from jax.experimental.pallas import tpu as pltpu

F32 = jnp.float32
BF16 = jnp.bfloat16

D_MODEL = 1024
N_HEADS = 8
HEAD_DIM = 64
ATTN_W = N_HEADS * HEAD_DIM
CONV_CH = D_MODEL // 2
D_FF = 2816
BLOCK = 256
TOP_K = 3
PAGE = 128
NUM_BUCKETS = 32
MAX_DISTANCE = 128
EPS = 1e-6
N_COLS = 3 * ATTN_W + 3 * CONV_CH + 2 * D_MODEL

FF_CHUNK = 256
N_FF_CHUNKS = D_FF // FF_CHUNK
TOKEN_TILE = 512
NEG_BIG = -1e30
VMEM_LIMIT = 60 * 1024 * 1024


def _bucket_thresholds():
    max_exact = NUM_BUCKETS // 2
    d = np.arange(0, 4 * MAX_DISTANCE)
    logd = np.log(np.maximum(d, 1) / max_exact)
    large = max_exact + (logd / math.log(MAX_DISTANCE / max_exact) * (NUM_BUCKETS - max_exact)).astype(np.int32)
    bucket = np.where(d < max_exact, d, np.minimum(large, NUM_BUCKETS - 1))
    return [int(np.argmax(bucket >= b)) for b in range(NUM_BUCKETS)]


BUCKET_THR = _bucket_thresholds()


def _dot(a, b):
    return jnp.dot(a, b, preferred_element_type=F32)


def _dot_nt(a, b):
    return lax.dot_general(a, b, (((1,), (1,)), ((), ())), preferred_element_type=F32)


def _const_spec(shape):
    nd = len(shape)
    return pl.BlockSpec(shape, lambda *_: (0,) * nd, pipeline_mode=pl.Buffered(1))


def _bias_from_distance(d, rb_ref, h, b_far):
    val = jnp.zeros(d.shape, F32)
    for b in range(NUM_BUCKETS - 2, -1, -1):
        val = jnp.where(d < BUCKET_THR[b + 1], rb_ref[b, h] - b_far, val)
    return val


def _bias_kernel(rb_ref, tile_ref, row_ref, own_ref):
    h = pl.program_id(0)
    b_far = rb_ref[NUM_BUCKETS - 1, h]
    j = lax.broadcasted_iota(jnp.int32, (BLOCK, BLOCK), 0)
    i = lax.broadcasted_iota(jnp.int32, (BLOCK, BLOCK), 1)
    tile_ref[0, 0] = jnp.where(j <= i, _bias_from_distance(i - j, rb_ref, h, b_far), NEG_BIG)
    tile_ref[0, 1] = _bias_from_distance(i - j + BLOCK, rb_ref, h, b_far)
    jr = lax.broadcasted_iota(jnp.int32, (1, BLOCK), 1)
    row_ref[0] = _bias_from_distance(BLOCK - jr, rb_ref, h, b_far)
    own_ref[0] = jnp.zeros((1, 128), F32) + (rb_ref[0, h] - b_far)


def _bias_tables(rel_bias):
    return pl.pallas_call(
        _bias_kernel,
        grid=(N_HEADS,),
        in_specs=[pl.BlockSpec(memory_space=pltpu.SMEM)],
        out_specs=[
            pl.BlockSpec((1, 2, BLOCK, BLOCK), lambda h: (h, 0, 0, 0)),
            pl.BlockSpec((1, 1, BLOCK), lambda h: (h, 0, 0)),
            pl.BlockSpec((1, 1, 128), lambda h: (h, 0, 0)),
        ],
        out_shape=[
            jax.ShapeDtypeStruct((N_HEADS, 2, BLOCK, BLOCK), F32),
            jax.ShapeDtypeStruct((N_HEADS, 1, BLOCK), F32),
            jax.ShapeDtypeStruct((N_HEADS, 1, 128), F32),
        ],
    )(rel_bias)


def _inproj_kernel(*refs, tm, sample, tiles_per_seq):
    if sample:
        (x_ref, n1_ref, w_ref, qn_ref, kn_ref, bd_ref, cw_ref, st0_ref, st1_ref,
         q_ref, k_ref, v_ref, sin_ref, ga_ref, gb_ref, u_ref) = refs
    else:
        (x_ref, n1_ref, w_ref, qn_ref, kn_ref, bd_ref, cw_ref,
         q_ref, kt_ref, vt_ref, kb_ref, vtb_ref, ksum_ref, sin_ref, ga_ref, gb_ref, cst_ref,
         ucar_ref) = refs

    x = x_ref[...]
    ms = jnp.mean(x * x, axis=-1, keepdims=True)
    xn = (x * lax.rsqrt(ms + EPS) * n1_ref[...]).astype(BF16)

    def proj(a, b):
        return _dot(xn, w_ref[:, a:b])

    def head_norm(t, w_row):
        sq = (t * t).astype(BF16)
        bd = bd_ref[...]
        half = ATTN_W // 2
        msq = jnp.concatenate([_dot(sq[:, :half], bd), _dot(sq[:, half:], bd)], axis=1)
        return t * lax.rsqrt(msq + EPS) * w_row

    q = head_norm(proj(0, ATTN_W), qn_ref[...]) * (HEAD_DIM ** -0.5)
    k = head_norm(proj(ATTN_W, 2 * ATTN_W), kn_ref[...])
    v = proj(2 * ATTN_W, 3 * ATTN_W)
    c0 = 3 * ATTN_W
    cb = proj(c0, c0 + CONV_CH)
    u = proj(c0 + CONV_CH, c0 + 2 * CONV_CH) * proj(c0 + 2 * CONV_CH, c0 + 3 * CONV_CH)
    g0 = c0 + 3 * CONV_CH
    ga_ref[...] = proj(g0, g0 + D_MODEL).astype(ga_ref.dtype)
    gb_ref[...] = proj(g0 + D_MODEL, g0 + 2 * D_MODEL).astype(gb_ref.dtype)

    w0, w1, w2 = cw_ref[0:1, :], cw_ref[1:2, :], cw_ref[2:3, :]
    if sample:
        q_ref[...] = q
        k_ref[...] = k
        v_ref[...] = v
        uc = st0_ref[...] * w0 + st1_ref[...] * w1 + u * w2
        u_ref[...] = u
    else:
        q_ref[...] = q.astype(BF16)
        kb_ref[...] = k.astype(BF16)
        kt_ref[0] = k.T
        v_t = v.T
        vt_ref[0] = v_t
        for r in range(tm // BLOCK):
            vtb_ref[0, r] = v_t[:, r * BLOCK:(r + 1) * BLOCK].astype(BF16)
            ksum_ref[0, r:r + 1, :] = jnp.sum(k[r * BLOCK:(r + 1) * BLOCK], axis=0, keepdims=True)

        @pl.when(pl.program_id(0) % tiles_per_seq == 0)
        def _():
            ucar_ref[...] = jnp.zeros_like(ucar_ref)

        ue = jnp.concatenate([ucar_ref[...], u], axis=0)
        uc = ue[6:6 + tm] * w0 + ue[7:7 + tm] * w1 + u * w2
        ucar_ref[...] = u[tm - 8:tm]
        cst_ref[0] = u[tm - 2:tm]
    sin_ref[...] = (cb * uc).astype(sin_ref.dtype)


def _inproj(x, n1, w_in, qn, kn, bd, conv_w, states, *, sample, seq_len=1):
    n_tok = x.shape[0]
    tm = n_tok if sample else TOKEN_TILE
    n_tiles = n_tok // tm
    tiles_per_seq = 1 if sample else seq_len // tm

    def row(width):
        return pl.BlockSpec((tm, width), lambda i: (i, 0))

    in_specs = [row(D_MODEL), _const_spec((1, D_MODEL)), _const_spec((D_MODEL, N_COLS)),
                _const_spec((1, ATTN_W)), _const_spec((1, ATTN_W)), _const_spec((BLOCK, BLOCK)),
                _const_spec((3, CONV_CH))]
    args = [x, n1, w_in, qn, kn, bd, conv_w]
    if sample:
        in_specs += [row(CONV_CH), row(CONV_CH)]
        args += list(states)
        out_specs = [row(ATTN_W), row(ATTN_W), row(ATTN_W), row(CONV_CH), row(D_MODEL), row(D_MODEL),
                     row(CONV_CH)]
        out_shape = [jax.ShapeDtypeStruct((n_tok, ATTN_W), F32)] * 3 + [
            jax.ShapeDtypeStruct((n_tok, CONV_CH), BF16),
            jax.ShapeDtypeStruct((n_tok, D_MODEL), BF16),
            jax.ShapeDtypeStruct((n_tok, D_MODEL), BF16),
            jax.ShapeDtypeStruct((n_tok, CONV_CH), F32)]
        scratch = []
    else:
        blocks_per_tile = tm // BLOCK
        n_seq = n_tiles // tiles_per_seq
        t_spec = pl.BlockSpec((1, ATTN_W, tm), lambda i: (i // tiles_per_seq, 0, i % tiles_per_seq))
        out_specs = [row(ATTN_W), t_spec, t_spec, row(ATTN_W),
                     pl.BlockSpec((1, blocks_per_tile, ATTN_W, BLOCK),
                                  lambda i: (i // tiles_per_seq, i % tiles_per_seq, 0, 0)),
                     pl.BlockSpec((1, blocks_per_tile, ATTN_W), lambda i: (i, 0, 0)),
                     row(CONV_CH), row(D_MODEL), row(D_MODEL),
                     pl.BlockSpec((1, 2, CONV_CH), lambda i: (i // tiles_per_seq, 0, 0))]
        out_shape = [jax.ShapeDtypeStruct((n_tok, ATTN_W), BF16),
                     jax.ShapeDtypeStruct((n_seq, ATTN_W, seq_len), F32),
                     jax.ShapeDtypeStruct((n_seq, ATTN_W, seq_len), F32),
                     jax.ShapeDtypeStruct((n_tok, ATTN_W), BF16),
                     jax.ShapeDtypeStruct((n_seq, seq_len // BLOCK, ATTN_W, BLOCK), BF16),
                     jax.ShapeDtypeStruct((n_tiles, blocks_per_tile, ATTN_W), F32),
                     jax.ShapeDtypeStruct((n_tok, CONV_CH), BF16),
                     jax.ShapeDtypeStruct((n_tok, D_MODEL), BF16),
                     jax.ShapeDtypeStruct((n_tok, D_MODEL), BF16),
                     jax.ShapeDtypeStruct((n_tiles // tiles_per_seq, 2, CONV_CH), F32)]
        scratch = [pltpu.VMEM((8, CONV_CH), F32)]
    return pl.pallas_call(
        functools.partial(_inproj_kernel, tm=tm, sample=sample, tiles_per_seq=tiles_per_seq),
        grid=(n_tiles,),
        in_specs=in_specs,
        out_specs=out_specs,
        out_shape=out_shape,
        scratch_shapes=scratch,
        compiler_params=pltpu.CompilerParams(dimension_semantics=("arbitrary",),
                                             vmem_limit_bytes=VMEM_LIMIT),
    )(*args)


STREAM_SLOTS = 3


def _block_diag_query(q_row):
    row = lax.broadcasted_iota(jnp.int32, (N_HEADS, ATTN_W), 0)
    lane_head = lax.broadcasted_iota(jnp.int32, (N_HEADS, ATTN_W), 1) // HEAD_DIM
    q_rows = jnp.where(row == lane_head, q_row, 0.0)
    q_hi = q_rows.astype(BF16).astype(F32)
    return jnp.concatenate([q_hi, q_rows - q_hi], axis=0).astype(BF16)


def _page_scores(q2, page):
    s2 = _dot(q2, page.reshape(ATTN_W, PAGE).astype(BF16))
    return s2[:N_HEADS] + s2[N_HEADS:]


def _attn_kernel(pt_ref, q_ref, k_ref, vt_ref, ksum_ref, bias_ref, qs_ref, ck_ref, o_ref, sc_ref,
                 pen_ref, m_ref, l_ref, acc_ref, pbuf, psem, *, pages_per_step):
    qb = pl.program_id(2)
    n_blocks = k_ref.shape[1] // BLOCK
    nq = 2 * BLOCK
    step = (pl.program_id(0) * pl.num_programs(1) + pl.program_id(1)) * pl.num_programs(2) + qb
    n_steps = pl.num_programs(0) * pl.num_programs(1) * pl.num_programs(2)

    def page_copies(step_, slot_):
        return [pltpu.make_async_copy(ck_ref.at[0, pt_ref[step_ * pages_per_step + i]], pbuf.at[slot_, i],
                                      psem.at[slot_]) for i in range(pages_per_step)]

    @pl.when(step == 0)
    def _():
        for ahead in range(STREAM_SLOTS - 1):
            for c in page_copies(ahead, ahead):
                c.start()

    @pl.when(step + (STREAM_SLOTS - 1) < n_steps)
    def _():
        nxt = step + (STREAM_SLOTS - 1)
        for c in page_copies(nxt, nxt % STREAM_SLOTS):
            c.start()

    q = q_ref[0]
    lane_head = lax.broadcasted_iota(jnp.int32, q.shape, 1) // HEAD_DIM
    zero = jnp.zeros_like(q)
    q2 = jnp.concatenate([jnp.where(lane_head == 0, q, zero), jnp.where(lane_head == 1, q, zero)], axis=0)

    kmean = ksum_ref[0] * (1.0 / BLOCK)
    km_hi = kmean.astype(BF16)
    km_lo = (kmean - km_hi.astype(F32)).astype(BF16)
    g2 = _dot_nt(jnp.concatenate([km_hi, km_lo], axis=0), q2)
    g = g2[:n_blocks] + g2[n_blocks:]
    blk = lax.broadcasted_iota(jnp.int32, (n_blocks, nq), 0)
    valid = blk < qb
    g = jnp.where(valid, g, -jnp.inf)
    rank = jnp.zeros(g.shape, jnp.int32)
    for j in range(n_blocks):
        gj = g[j:j + 1, :]
        beats = (gj > g) | ((gj == g) & (blk > j))
        rank = rank + beats.astype(jnp.int32)
    sel = valid & (rank < TOP_K)
    pen_ref[...] = jnp.where(sel, 0.0, NEG_BIG)

    def scores(kb):
        kblk = k_ref[0, pl.ds(pl.multiple_of(kb * BLOCK, BLOCK), BLOCK), :]
        return _dot_nt(kblk, q2)

    def penalty(kb):
        return pen_ref[pl.ds(kb, 1), :]

    def pv(kb, p):
        vt = vt_ref[0, kb]
        return jnp.concatenate([_dot(vt[:HEAD_DIM], p[:, :BLOCK]), _dot(vt[HEAD_DIM:], p[:, BLOCK:])], axis=1)

    def bias_tile(t):
        return jnp.concatenate([bias_ref[0, t], bias_ref[1, t]], axis=1)

    s = scores(qb) + bias_tile(0)
    m = jnp.max(s, axis=0, keepdims=True)
    p = jnp.exp(s - m)
    m_ref[...] = m
    l_ref[...] = jnp.sum(p, axis=0, keepdims=True)
    acc_ref[...] = pv(qb, p.astype(BF16))

    def update(blocks):
        m_old = m_ref[...]
        m_new = m_old
        for _, s_ in blocks:
            m_new = jnp.maximum(m_new, jnp.max(s_, axis=0, keepdims=True))
        alpha = jnp.exp(m_old - m_new)
        l_new = alpha * l_ref[...]
        acc_new = alpha * acc_ref[...]
        for kb_, s_ in blocks:
            p_ = jnp.exp(s_ - m_new)
            l_new = l_new + jnp.sum(p_, axis=0, keepdims=True)
            acc_new = acc_new + pv(kb_, p_.astype(BF16))
        m_ref[...] = m_new
        l_ref[...] = l_new
        acc_ref[...] = acc_new

    n_far = jnp.maximum(qb - 1, 0)

    def far_pair(i, carry):
        kb = 2 * i
        update([(kb, scores(kb) + penalty(kb)), (kb + 1, scores(kb + 1) + penalty(kb + 1))])
        return carry

    lax.fori_loop(0, n_far // 2, far_pair, 0)

    def prev_scores():
        return scores(qb - 1) + penalty(qb - 1) + bias_tile(1)

    @pl.when((qb >= 1) & (n_far % 2 == 0))
    def _():
        update([(qb - 1, prev_scores())])

    @pl.when((qb >= 1) & (n_far % 2 == 1))
    def _():
        kb = n_far - 1
        update([(kb, scores(kb) + penalty(kb)), (qb - 1, prev_scores())])

    out = acc_ref[...] / l_ref[...]
    out_t = jnp.concatenate([out[:, :BLOCK], out[:, BLOCK:]], axis=0)
    o_ref[0] = out_t.T.astype(o_ref.dtype)

    slot = step % STREAM_SLOTS
    for c in page_copies(step, slot):
        c.wait()
    q2s = _block_diag_query(qs_ref[0])
    for i in range(pages_per_step):
        sc_ref[i] = _page_scores(q2s, pbuf[slot, i])


def _prompt_attention(q, k, vt_blocks, ksum, bias_tiles, page_table_flat, q_sample, cache_kt, n_pages):
    b, s, _ = q.shape
    n_blocks = s // BLOCK
    n_hp = N_HEADS // 2
    n_steps = b * n_hp * n_blocks
    n_seq = q_sample.shape[0]
    total_pages = n_seq * n_pages
    pages_per_step = total_pages // n_steps
    assert pages_per_step * n_steps == total_pages and n_pages % pages_per_step == 0

    def step_of(bi, hp, qb):
        return (bi * n_hp + hp) * n_blocks + qb

    attn, scores = pl.pallas_call(
        functools.partial(_attn_kernel, pages_per_step=pages_per_step),
        grid_spec=pltpu.PrefetchScalarGridSpec(
            num_scalar_prefetch=1,
            grid=(b, n_hp, n_blocks),
            in_specs=[
                pl.BlockSpec((1, BLOCK, 128), lambda bi, hp, qb, pt: (bi, qb, hp)),
                pl.BlockSpec((1, s, 128), lambda bi, hp, qb, pt: (bi, 0, hp)),
                pl.BlockSpec((1, n_blocks, 128, BLOCK), lambda bi, hp, qb, pt: (bi, 0, hp, 0)),
                pl.BlockSpec((1, n_blocks, 128), lambda bi, hp, qb, pt: (bi, 0, hp)),
                pl.BlockSpec((2, 2, BLOCK, BLOCK), lambda bi, hp, qb, pt: (hp, 0, 0, 0)),
                pl.BlockSpec((1, 1, ATTN_W),
                             lambda bi, hp, qb, pt: (step_of(bi, hp, qb) * pages_per_step // n_pages, 0, 0)),
                pl.BlockSpec(memory_space=pl.ANY),
            ],
            out_specs=[
                pl.BlockSpec((1, BLOCK, 128), lambda bi, hp, qb, pt: (bi, qb, hp)),
                pl.BlockSpec((pages_per_step, N_HEADS, PAGE), lambda bi, hp, qb, pt: (step_of(bi, hp, qb), 0, 0)),
            ],
            scratch_shapes=[
                pltpu.VMEM((n_blocks, 2 * BLOCK), F32),
                pltpu.VMEM((1, 2 * BLOCK), F32),
                pltpu.VMEM((1, 2 * BLOCK), F32),
                pltpu.VMEM((HEAD_DIM, 2 * BLOCK), F32),
                pltpu.VMEM((STREAM_SLOTS, pages_per_step, N_HEADS, HEAD_DIM, PAGE), F32),
                pltpu.SemaphoreType.DMA((STREAM_SLOTS,)),
            ],
        ),
        out_shape=[jax.ShapeDtypeStruct((b, s, ATTN_W), BF16),
                   jax.ShapeDtypeStruct((total_pages, N_HEADS, PAGE), F32)],
        compiler_params=pltpu.CompilerParams(
            dimension_semantics=("arbitrary", "arbitrary", "arbitrary"),
            vmem_limit_bytes=VMEM_LIMIT),
    )(page_table_flat, q, k, vt_blocks, ksum, bias_tiles, q_sample, cache_kt)
    return attn, scores.reshape(n_seq, n_pages, N_HEADS, PAGE)


def _merge_ffn_kernel(*refs, tm, sample, tiles_per_seq):
    if sample:
        (x_ref, at_ref, sin_ref, ga_ref, gb_ref, wa_ref, wc_ref, wo_ref, n2_ref, wup_ref, fcw_ref,
         wdn_ref, fs0_ref, fs1_ref, y_ref, g_ref, acc_ref) = refs
    else:
        (x_ref, at_ref, sin_ref, ga_ref, gb_ref, wa_ref, wc_ref, wo_ref, n2_ref, wup_ref, fcw_ref,
         wdn_ref, y_ref, fst_ref, acc_ref, gcar_ref) = refs

    a = _dot(at_ref[...].astype(BF16), wa_ref[...])
    b = _dot(sin_ref[...], wc_ref[...])
    merged = jax.nn.sigmoid(ga_ref[...].astype(F32)) * a + jax.nn.sigmoid(gb_ref[...].astype(F32)) * b
    x1 = x_ref[...] + _dot(merged.astype(BF16), wo_ref[...])
    ms = jnp.mean(x1 * x1, axis=-1, keepdims=True)
    xn2 = (x1 * lax.rsqrt(ms + EPS) * n2_ref[...]).astype(BF16)
    acc_ref[...] = x1

    if not sample:
        @pl.when(pl.program_id(0) % tiles_per_seq == 0)
        def _():
            gcar_ref[...] = jnp.zeros_like(gcar_ref)

    for c in range(N_FF_CHUNKS):
        cols = slice(c * FF_CHUNK, (c + 1) * FF_CHUNK)
        g = _dot(xn2, wup_ref[:, cols])
        u = _dot(xn2, wup_ref[:, D_FF + c * FF_CHUNK:D_FF + (c + 1) * FF_CHUNK])
        w0, w1, w2 = fcw_ref[0:1, cols], fcw_ref[1:2, cols], fcw_ref[2:3, cols]
        if sample:
            gc = fs0_ref[:, cols] * w0 + fs1_ref[:, cols] * w1 + g * w2
            g_ref[:, cols] = g
        else:
            ge = jnp.concatenate([gcar_ref[:, cols], g], axis=0)
            gc = ge[6:6 + tm] * w0 + ge[7:7 + tm] * w1 + g * w2
            gcar_ref[:, cols] = g[tm - 8:tm]
            fst_ref[0, :, cols] = g[tm - 2:tm]
        hidden = (gc * jax.nn.sigmoid(gc) * u).astype(BF16)
        acc_ref[...] += _dot(hidden, wdn_ref[cols, :])
    y_ref[...] = acc_ref[...]


def _merge_ffn(x, attn, s_in, ga, gb, wa, wc, wo, n2, wup, fcw, wdn, states, *, sample, seq_len=1):
    n_tok = x.shape[0]
    tm = n_tok if sample else TOKEN_TILE
    n_tiles = n_tok // tm
    tiles_per_seq = 1 if sample else seq_len // tm

    def row(width):
        return pl.BlockSpec((tm, width), lambda i: (i, 0))

    in_specs = [row(D_MODEL), row(ATTN_W), row(CONV_CH), row(D_MODEL), row(D_MODEL),
                _const_spec((ATTN_W, D_MODEL)), _const_spec((CONV_CH, D_MODEL)),
                _const_spec((D_MODEL, D_MODEL)), _const_spec((1, D_MODEL)),
                _const_spec((D_MODEL, 2 * D_FF)), _const_spec((3, D_FF)), _const_spec((D_FF, D_MODEL))]
    args = [x, attn, s_in, ga, gb, wa, wc, wo, n2, wup, fcw, wdn]
    scratch = [pltpu.VMEM((tm, D_MODEL), F32)]
    if sample:
        in_specs += [row(D_FF), row(D_FF)]
        args += list(states)
        out_specs = [row(D_MODEL), row(D_FF)]
        out_shape = [jax.ShapeDtypeStruct((n_tok, D_MODEL), F32),
                     jax.ShapeDtypeStruct((n_tok, D_FF), F32)]
    else:
        n_seq = n_tiles // tiles_per_seq
        out_specs = [row(D_MODEL), pl.BlockSpec((1, 2, D_FF), lambda i: (i // tiles_per_seq, 0, 0))]
        out_shape = [jax.ShapeDtypeStruct((n_tok, D_MODEL), F32),
                     jax.ShapeDtypeStruct((n_seq, 2, D_FF), F32)]
        scratch.append(pltpu.VMEM((8, D_FF), F32))
    return pl.pallas_call(
        functools.partial(_merge_ffn_kernel, tm=tm, sample=sample, tiles_per_seq=tiles_per_seq),
        grid=(n_tiles,),
        in_specs=in_specs,
        out_specs=out_specs,
        out_shape=out_shape,
        scratch_shapes=scratch,
        compiler_params=pltpu.CompilerParams(dimension_semantics=("arbitrary",),
                                             vmem_limit_bytes=VMEM_LIMIT),
    )(*args)


SEQ_PER_SELECT_STEP = 8


def _select_kernel(sc_ref, idx_ref):
    n_pages = sc_ref.shape[1]
    pages_per_block = BLOCK // PAGE
    n_blocks = n_pages // pages_per_block
    blk = lax.broadcasted_iota(jnp.int32, (n_blocks, N_HEADS, 128), 0)
    for s in range(SEQ_PER_SELECT_STEP):
        page_sum = jnp.sum(sc_ref[s], axis=-1, keepdims=True)
        block_sum = jnp.sum(page_sum.reshape(n_blocks, pages_per_block, N_HEADS, 1), axis=1)
        gate = jnp.broadcast_to(block_sum * (1.0 / BLOCK), (n_blocks, N_HEADS, 128))
        for r in range(TOP_K):
            best = jnp.max(gate, axis=0, keepdims=True)
            idx = jnp.min(jnp.where(gate == best, blk, n_blocks), axis=0, keepdims=True)
            idx_ref[s, r] = idx[0]
            gate = jnp.where(blk == idx, -jnp.inf, gate)


def _select_blocks(scores):
    n_seq, n_pages = scores.shape[:2]
    return pl.pallas_call(
        _select_kernel,
        grid=(n_seq // SEQ_PER_SELECT_STEP,),
        in_specs=[pl.BlockSpec((SEQ_PER_SELECT_STEP, n_pages, N_HEADS, PAGE), lambda i: (i, 0, 0, 0))],
        out_specs=pl.BlockSpec((SEQ_PER_SELECT_STEP, TOP_K, N_HEADS, 128), lambda i: (i, 0, 0, 0)),
        out_shape=jax.ShapeDtypeStruct((n_seq, TOP_K, N_HEADS, 128), jnp.int32),
    )(scores)


PAGES_PER_BLOCK = BLOCK // PAGE
CHUNKS_PER_HEAD = TOP_K * PAGES_PER_BLOCK
N_CHUNKS = N_HEADS * CHUNKS_PER_HEAD
KEYS_PER_HEAD = CHUNKS_PER_HEAD * PAGE


def _sample_attn_kernel(pt_ref, idx_ref, sc_ref, q_ref, kn_ref, vn_ref, brow_ref, ownb_ref, cv_ref,
                        o_ref, vbuf, logit_ref, sem, *, n_seq, n_pages):
    b = pl.program_id(0)
    slot = b % 2

    def picked_page(seq, h, r, j):
        n = idx_ref[(seq * TOP_K + r) * N_HEADS + h]
        return n, n * PAGES_PER_BLOCK + j

    def copies(seq, slot_):
        out = []
        for h in range(N_HEADS):
            for r in range(TOP_K):
                for j in range(PAGES_PER_BLOCK):
                    _, logical = picked_page(seq, h, r, j)
                    page = pt_ref[seq * n_pages + logical]
                    c = (h * TOP_K + r) * PAGES_PER_BLOCK + j
                    out.append(pltpu.make_async_copy(cv_ref.at[0, page, h], vbuf.at[slot_, c], sem.at[slot_]))
        return out

    @pl.when(b == 0)
    def _():
        for c in copies(0, 0):
            c.start()

    @pl.when(b + 1 < n_seq)
    def _():
        for c in copies(b + 1, 1 - slot):
            c.start()

    last_block = n_pages // PAGES_PER_BLOCK - 1
    for h in range(N_HEADS):
        for r in range(TOP_K):
            for j in range(PAGES_PER_BLOCK):
                n, logical = picked_page(b, h, r, j)
                piece = sc_ref[0, logical][h:h + 1, :]
                near = brow_ref[h][:, j * PAGE:(j + 1) * PAGE]
                piece = piece + jnp.where(n == last_block, near, jnp.zeros_like(near))
                c = r * PAGES_PER_BLOCK + j
                logit_ref[h:h + 1, c * PAGE:(c + 1) * PAGE] = piece
    logits = logit_ref[...]

    q = q_ref[0]
    own = jnp.sum(q * kn_ref[0], axis=-1, keepdims=True) + ownb_ref[:, 0, 0:1]
    m = jnp.maximum(jnp.max(logits, axis=-1, keepdims=True), own)
    p = jnp.exp(logits - m)
    p_own = jnp.exp(own - m)
    denom = jnp.sum(p, axis=-1, keepdims=True) + p_own

    p_wide = jnp.concatenate([p] * N_HEADS, axis=1)
    col = lax.broadcasted_iota(jnp.int32, p_wide.shape, 1)
    row = lax.broadcasted_iota(jnp.int32, p_wide.shape, 0)
    p_wide = jnp.where(col // KEYS_PER_HEAD == row, p_wide, 0.0)
    p16 = jnp.concatenate([p_wide, jnp.zeros_like(p_wide)], axis=0).astype(BF16)

    for c in copies(b, slot):
        c.wait()
    vt = jnp.concatenate([vbuf[slot, c] for c in range(N_CHUNKS)], axis=1).astype(BF16)
    ctx = _dot_nt(p16, vt)[:N_HEADS] + p_own * vn_ref[0]
    o_ref[0] = ctx / denom


def _sample_attention(page_table_flat, idx_flat, scores, q_heads, k_new, v_new, brow, ownb, cache_vt, n_pages):
    n_seq = q_heads.shape[0]
    head_spec = pl.BlockSpec((1, N_HEADS, HEAD_DIM), lambda b, pt, ix: (b, 0, 0))
    return pl.pallas_call(
        functools.partial(_sample_attn_kernel, n_seq=n_seq, n_pages=n_pages),
        grid_spec=pltpu.PrefetchScalarGridSpec(
            num_scalar_prefetch=2,
            grid=(n_seq,),
            in_specs=[pl.BlockSpec((1, n_pages, N_HEADS, PAGE), lambda b, pt, ix: (b, 0, 0, 0)),
                      head_spec, head_spec, head_spec,
                      pl.BlockSpec((N_HEADS, 1, BLOCK), lambda b, pt, ix: (0, 0, 0)),
                      pl.BlockSpec((N_HEADS, 1, 128), lambda b, pt, ix: (0, 0, 0)),
                      pl.BlockSpec(memory_space=pl.ANY)],
            out_specs=head_spec,
            scratch_shapes=[pltpu.VMEM((2, N_CHUNKS, HEAD_DIM, PAGE), F32),
                            pltpu.VMEM((N_HEADS, KEYS_PER_HEAD), F32),
                            pltpu.SemaphoreType.DMA((2,))],
        ),
        out_shape=jax.ShapeDtypeStruct((n_seq, N_HEADS, HEAD_DIM), F32),
        compiler_params=pltpu.CompilerParams(dimension_semantics=("arbitrary",),
                                             vmem_limit_bytes=VMEM_LIMIT),
    )(page_table_flat, idx_flat, scores, q_heads, k_new, v_new, brow, ownb, cache_vt)


def kernel(x_prompt, x_sample, cache_k, cache_v, page_table, state_conv, state_ffn, norm1_w, w_in, q_norm_w, k_norm_w, conv_w, w_attn_up, w_conv_out, w_o, norm2_w, w_ffn_up, ffn_conv_w, w_ffn_down, rel_bias):
    batch, seq, _ = x_prompt.shape
    n_seq = x_sample.shape[0]
    n_pages = page_table.shape[1]

    w_in_b = w_in[0].astype(BF16)
    wa = w_attn_up[0].astype(BF16)
    wc = w_conv_out[0].astype(BF16)
    wo = w_o[0].astype(BF16)
    wup = w_ffn_up[0].astype(BF16)
    wdn = w_ffn_down[0].astype(BF16)
    fcw = ffn_conv_w[0]
    n1 = norm1_w[0].reshape(1, D_MODEL)
    n2 = norm2_w[0].reshape(1, D_MODEL)
    qn = jnp.tile(q_norm_w[0], N_HEADS).reshape(1, ATTN_W)
    kn = jnp.tile(k_norm_w[0], N_HEADS).reshape(1, ATTN_W)
    lane = np.arange(BLOCK)
    bd = jnp.asarray((lane[:, None] // HEAD_DIM == lane[None, :] // HEAD_DIM) / HEAD_DIM, BF16)
    cw = conv_w[0]

    bias_tiles, bias_row, bias_own = _bias_tables(rel_bias)

    xp = x_prompt.reshape(batch * seq, D_MODEL)
    (q_b, kt_p, vt_p, k_b, vt_b, ksum_p, sin_p, ga_p, gb_p, conv_p) = _inproj(
        xp, n1, w_in_b, qn, kn, bd, cw, None, sample=False, seq_len=seq)
    xs = x_sample.reshape(n_seq, D_MODEL)
    (q_s, k_s, v_s, sin_s, ga_s, gb_s, u_s) = _inproj(
        xs, n1, w_in_b, qn, kn, bd, cw, (state_conv[0, :, 0, :], state_conv[0, :, 1, :]), sample=True)
    pt_flat = page_table.reshape(-1)
    cache_kt = cache_k.transpose(0, 1, 3, 4, 2)
    cache_vt = cache_v.transpose(0, 1, 3, 4, 2)

    attn_p, scores = _prompt_attention(q_b.reshape(batch, seq, ATTN_W), k_b.reshape(batch, seq, ATTN_W), vt_b,
                                       ksum_p.reshape(batch, seq // BLOCK, ATTN_W), bias_tiles,
                                       pt_flat, q_s.reshape(n_seq, 1, ATTN_W), cache_kt, n_pages)
    y_p, ffn_p = _merge_ffn(xp, attn_p.reshape(batch * seq, ATTN_W), sin_p, ga_p, gb_p,
                            wa, wc, wo, n2, wup, fcw, wdn, None, sample=False, seq_len=seq)

    idx = _select_blocks(scores)[..., 0]
    attn_s = _sample_attention(pt_flat, idx.reshape(-1), scores, q_s.reshape(n_seq, N_HEADS, HEAD_DIM),
                               k_s.reshape(n_seq, N_HEADS, HEAD_DIM), v_s.reshape(n_seq, N_HEADS, HEAD_DIM),
                               bias_row, bias_own, cache_vt, n_pages)

    y_s, g_s = _merge_ffn(xs, attn_s.reshape(n_seq, ATTN_W), sin_s, ga_s, gb_s,
                          wa, wc, wo, n2, wup, fcw, wdn,
                          (state_ffn[0, :, 0, :], state_ffn[0, :, 1, :]), sample=True)

    return (
        y_p.reshape(batch, seq, D_MODEL),
        y_s.reshape(n_seq, 1, D_MODEL),
        kt_p.reshape(1, batch, N_HEADS, HEAD_DIM, seq).transpose(0, 1, 4, 2, 3),
        vt_p.reshape(1, batch, N_HEADS, HEAD_DIM, seq).transpose(0, 1, 4, 2, 3),
        conv_p[None],
        ffn_p[None],
        k_s.reshape(1, n_seq, 1, N_HEADS, HEAD_DIM),
        v_s.reshape(1, n_seq, 1, N_HEADS, HEAD_DIM),
        jnp.stack([state_conv[0, :, 1, :], u_s], axis=1)[None],
        jnp.stack([state_ffn[0, :, 1, :], g_s], axis=1)[None],
    )
```

```python
import functools
import math

import numpy as np
import jax
import jax.numpy as jnp
from jax import lax
from jax.experimental import pallas as pl
from jax.experimental.pallas import tpu as pltpu

F32 = jnp.float32
BF16 = jnp.bfloat16

D_MODEL = 1024
N_HEADS = 8
HEAD_DIM = 64
ATTN_W = N_HEADS * HEAD_DIM
CONV_CH = D_MODEL // 2
D_FF = 2816
BLOCK = 256
TOP_K = 3
PAGE = 128
NUM_BUCKETS = 32
MAX_DISTANCE = 128
EPS = 1e-6
N_COLS = 3 * ATTN_W + 3 * CONV_CH + 2 * D_MODEL

FF_CHUNK = 256
N_FF_CHUNKS = D_FF // FF_CHUNK
TOKEN_TILE = 512
NEG_BIG = -1e30
VMEM_LIMIT = 60 * 1024 * 1024
LOG2E = math.log2(math.e)


def _bucket_thresholds():
    max_exact = NUM_BUCKETS // 2
    d = np.arange(0, 4 * MAX_DISTANCE)
    logd = np.log(np.maximum(d, 1) / max_exact)
    large = max_exact + (logd / math.log(MAX_DISTANCE / max_exact) * (NUM_BUCKETS - max_exact)).astype(np.int32)
    bucket = np.where(d < max_exact, d, np.minimum(large, NUM_BUCKETS - 1))
    return [int(np.argmax(bucket >= b)) for b in range(NUM_BUCKETS)]


BUCKET_THR = _bucket_thresholds()


def _dot(a, b):
    return jnp.dot(a, b, preferred_element_type=F32)


def _dot_nt(a, b):
    return lax.dot_general(a, b, (((1,), (1,)), ((), ())), preferred_element_type=F32)


def _const_spec(shape):
    nd = len(shape)
    return pl.BlockSpec(shape, lambda *_: (0,) * nd, pipeline_mode=pl.Buffered(1))


def _bias_from_distance(d, rb_ref, h, b_far):
    val = jnp.zeros(d.shape, F32)
    for b in range(NUM_BUCKETS - 2, -1, -1):
        val = jnp.where(d < BUCKET_THR[b + 1], rb_ref[b, h] - b_far, val)
    return val


def _bias_kernel(rb_ref, tile_ref, row_ref, own_ref):
    hp = pl.program_id(0)
    j = lax.broadcasted_iota(jnp.int32, (BLOCK, BLOCK), 0)
    i = lax.broadcasted_iota(jnp.int32, (BLOCK, BLOCK), 1)
    jr = lax.broadcasted_iota(jnp.int32, (1, BLOCK), 1)
    for e in range(2):
        h = 2 * hp + e
        b_far = rb_ref[NUM_BUCKETS - 1, h]
        lanes = slice(e * BLOCK, (e + 1) * BLOCK)
        tile_ref[0, 0, :, lanes] = jnp.where(j <= i, _bias_from_distance(i - j, rb_ref, h, b_far) * LOG2E, NEG_BIG)
        tile_ref[0, 1, :, lanes] = _bias_from_distance(i - j + BLOCK, rb_ref, h, b_far) * LOG2E
        row_ref[e] = _bias_from_distance(BLOCK - jr, rb_ref, h, b_far)
        own_ref[e] = jnp.zeros((1, 128), F32) + (rb_ref[0, h] - b_far)
    tile_ref[0, 2] = jnp.zeros((BLOCK, 2 * BLOCK), F32)


def _bias_tables(rel_bias):
    n_hp = N_HEADS // 2
    return pl.pallas_call(
        _bias_kernel,
        grid=(n_hp,),
        in_specs=[pl.BlockSpec(memory_space=pltpu.SMEM)],
        out_specs=[
            pl.BlockSpec((1, 3, BLOCK, 2 * BLOCK), lambda hp: (hp, 0, 0, 0)),
            pl.BlockSpec((2, 1, BLOCK), lambda hp: (hp, 0, 0)),
            pl.BlockSpec((2, 1, 128), lambda hp: (hp, 0, 0)),
        ],
        out_shape=[
            jax.ShapeDtypeStruct((n_hp, 3, BLOCK, 2 * BLOCK), F32),
            jax.ShapeDtypeStruct((N_HEADS, 1, BLOCK), F32),
            jax.ShapeDtypeStruct((N_HEADS, 1, 128), F32),
        ],
    )(rel_bias)


def _inproj_kernel(*refs, tm, sample, tiles_per_seq):
    if sample:
        (x_ref, n1_ref, w_ref, qn_ref, kn_ref, bd_ref, cw_ref, st0_ref, st1_ref,
         q_ref, k_ref, v_ref, sin_ref, ga_ref, gb_ref, u_ref) = refs
    else:
        (x_ref, n1_ref, w_ref, qn_ref, kn_ref, bd_ref, cw_ref,
         q_ref, kt_ref, vt_ref, kb_ref, vtb_ref, ksum_ref, sin_ref, ga_ref, gb_ref, cst_ref,
         ucar_ref) = refs

    x = x_ref[...]
    ms = jnp.mean(x * x, axis=-1, keepdims=True)
    xn = (x * lax.rsqrt(ms + EPS) * n1_ref[...]).astype(BF16)

    def proj(a, b):
        return _dot(xn, w_ref[:, a:b])

    def head_norm(t, w_row):
        sq = (t * t).astype(BF16)
        bd = bd_ref[...]
        half = ATTN_W // 2
        msq = jnp.concatenate([_dot(sq[:, :half], bd), _dot(sq[:, half:], bd)], axis=1)
        return t * lax.rsqrt(msq + EPS) * w_row

    q = head_norm(proj(0, ATTN_W), qn_ref[...]) * (HEAD_DIM ** -0.5)
    k = head_norm(proj(ATTN_W, 2 * ATTN_W), kn_ref[...])
    v = proj(2 * ATTN_W, 3 * ATTN_W)
    c0 = 3 * ATTN_W
    cb = proj(c0, c0 + CONV_CH)
    u = proj(c0 + CONV_CH, c0 + 2 * CONV_CH) * proj(c0 + 2 * CONV_CH, c0 + 3 * CONV_CH)
    g0 = c0 + 3 * CONV_CH
    ga_ref[...] = proj(g0, g0 + D_MODEL).astype(ga_ref.dtype)
    gb_ref[...] = proj(g0 + D_MODEL, g0 + 2 * D_MODEL).astype(gb_ref.dtype)

    w0, w1, w2 = cw_ref[0:1, :], cw_ref[1:2, :], cw_ref[2:3, :]
    if sample:
        q_ref[...] = q
        k_ref[...] = k
        v_ref[...] = v
        uc = st0_ref[...] * w0 + st1_ref[...] * w1 + u * w2
        u_ref[...] = u
    else:
        q_ref[...] = (q * LOG2E).astype(BF16)
        kb_ref[...] = k.astype(BF16)
        kt_ref[0] = k.T
        v_t = v.T
        vt_ref[0] = v_t
        for r in range(tm // BLOCK):
            vtb_ref[0, r] = v_t[:, r * BLOCK:(r + 1) * BLOCK].astype(BF16)
            ksum_ref[0, r:r + 1, :] = jnp.sum(k[r * BLOCK:(r + 1) * BLOCK], axis=0, keepdims=True)

        @pl.when(pl.program_id(0) % tiles_per_seq == 0)
        def _():
            ucar_ref[...] = jnp.zeros_like(ucar_ref)

        ue = jnp.concatenate([ucar_ref[...], u], axis=0)
        uc = ue[6:6 + tm] * w0 + ue[7:7 + tm] * w1 + u * w2
        ucar_ref[...] = u[tm - 8:tm]
        cst_ref[0] = u[tm - 2:tm]
    sin_ref[...] = (cb * uc).astype(sin_ref.dtype)


def _inproj(x, n1, w_in, qn, kn, bd, conv_w, states, *, sample, seq_len=1):
    n_tok = x.shape[0]
    tm = n_tok if sample else TOKEN_TILE
    n_tiles = n_tok // tm
    tiles_per_seq = 1 if sample else seq_len // tm

    def row(width):
        return pl.BlockSpec((tm, width), lambda i: (i, 0))

    in_specs = [row(D_MODEL), _const_spec((1, D_MODEL)), _const_spec((D_MODEL, N_COLS)),
                _const_spec((1, ATTN_W)), _const_spec((1, ATTN_W)), _const_spec((BLOCK, BLOCK)),
                _const_spec((3, CONV_CH))]
    args = [x, n1, w_in, qn, kn, bd, conv_w]
    if sample:
        in_specs += [row(CONV_CH), row(CONV_CH)]
        args += list(states)
        out_specs = [row(ATTN_W), row(ATTN_W), row(ATTN_W), row(CONV_CH), row(D_MODEL), row(D_MODEL),
                     row(CONV_CH)]
        out_shape = [jax.ShapeDtypeStruct((n_tok, ATTN_W), F32)] * 3 + [
            jax.ShapeDtypeStruct((n_tok, CONV_CH), BF16),
            jax.ShapeDtypeStruct((n_tok, D_MODEL), BF16),
            jax.ShapeDtypeStruct((n_tok, D_MODEL), BF16),
            jax.ShapeDtypeStruct((n_tok, CONV_CH), F32)]
        scratch = []
    else:
        blocks_per_tile = tm // BLOCK
        n_seq = n_tiles // tiles_per_seq
        t_spec = pl.BlockSpec((1, ATTN_W, tm), lambda i: (i // tiles_per_seq, 0, i % tiles_per_seq))
        out_specs = [row(ATTN_W), t_spec, t_spec, row(ATTN_W),
                     pl.BlockSpec((1, blocks_per_tile, ATTN_W, BLOCK),
                                  lambda i: (i // tiles_per_seq, i % tiles_per_seq, 0, 0)),
                     pl.BlockSpec((1, blocks_per_tile, ATTN_W), lambda i: (i, 0, 0)),
                     row(CONV_CH), row(D_MODEL), row(D_MODEL),
                     pl.BlockSpec((1, 2, CONV_CH), lambda i: (i // tiles_per_seq, 0, 0))]
        out_shape = [jax.ShapeDtypeStruct((n_tok, ATTN_W), BF16),
                     jax.ShapeDtypeStruct((n_seq, ATTN_W, seq_len), F32),
                     jax.ShapeDtypeStruct((n_seq, ATTN_W, seq_len), F32),
                     jax.ShapeDtypeStruct((n_tok, ATTN_W), BF16),
                     jax.ShapeDtypeStruct((n_seq, seq_len // BLOCK, ATTN_W, BLOCK), BF16),
                     jax.ShapeDtypeStruct((n_tiles, blocks_per_tile, ATTN_W), F32),
                     jax.ShapeDtypeStruct((n_tok, CONV_CH), BF16),
                     jax.ShapeDtypeStruct((n_tok, D_MODEL), BF16),
                     jax.ShapeDtypeStruct((n_tok, D_MODEL), BF16),
                     jax.ShapeDtypeStruct((n_tiles // tiles_per_seq, 2, CONV_CH), F32)]
        scratch = [pltpu.VMEM((8, CONV_CH), F32)]
    return pl.pallas_call(
        functools.partial(_inproj_kernel, tm=tm, sample=sample, tiles_per_seq=tiles_per_seq),
        grid=(n_tiles,),
        in_specs=in_specs,
        out_specs=out_specs,
        out_shape=out_shape,
        scratch_shapes=scratch,
        compiler_params=pltpu.CompilerParams(dimension_semantics=("arbitrary",),
                                             vmem_limit_bytes=VMEM_LIMIT),
    )(*args)


STREAM_SLOTS = 3


def _block_diag_query(q_row):
    row = lax.broadcasted_iota(jnp.int32, (N_HEADS, ATTN_W), 0)
    lane_head = lax.broadcasted_iota(jnp.int32, (N_HEADS, ATTN_W), 1) // HEAD_DIM
    q_rows = jnp.where(row == lane_head, q_row, 0.0)
    q_hi = q_rows.astype(BF16).astype(F32)
    return jnp.concatenate([q_hi, q_rows - q_hi], axis=0).astype(BF16)


def _page_scores(q2, page):
    s2 = _dot(q2, page.reshape(ATTN_W, PAGE).astype(BF16))
    return s2[:N_HEADS] + s2[N_HEADS:]


def _attn_kernel(pt_ref, q_ref, k_ref, vt_ref, ksum_ref, bias_ref, qs_ref, ck_ref, o_ref, sc_ref,
                 pen_ref, m_ref, l_ref, alpha_ref, acc_ref, s_ref, p_ref, pbuf, psem, *, pages_per_step):
    qb = pl.program_id(2)
    n_blocks = k_ref.shape[1] // BLOCK
    nq = 2 * BLOCK
    step = (pl.program_id(0) * pl.num_programs(1) + pl.program_id(1)) * pl.num_programs(2) + qb
    n_steps = pl.num_programs(0) * pl.num_programs(1) * pl.num_programs(2)

    def page_copies(step_, slot_):
        return [pltpu.make_async_copy(ck_ref.at[0, pt_ref[step_ * pages_per_step + i]], pbuf.at[slot_, i],
                                      psem.at[slot_]) for i in range(pages_per_step)]

    @pl.when(step == 0)
    def _():
        for ahead in range(STREAM_SLOTS - 1):
            for c in page_copies(ahead, ahead):
                c.start()

    @pl.when(step + (STREAM_SLOTS - 1) < n_steps)
    def _():
        nxt = step + (STREAM_SLOTS - 1)
        for c in page_copies(nxt, nxt % STREAM_SLOTS):
            c.start()

    slot = step % STREAM_SLOTS
    pltpu.make_async_copy(ck_ref.at[0, pl.ds(0, pages_per_step)], pbuf.at[slot], psem.at[slot]).wait()
    q2s = _block_diag_query(qs_ref[0])
    pages_at_head = (pages_per_step * 5) // 8
    for i in range(pages_at_head):
        sc_ref[i] = _page_scores(q2s, pbuf[slot, i])

    q = q_ref[0]
    lane_head = lax.broadcasted_iota(jnp.int32, q.shape, 1) // HEAD_DIM
    zero = jnp.zeros_like(q)
    q2 = jnp.concatenate([jnp.where(lane_head == 0, q, zero), jnp.where(lane_head == 1, q, zero)], axis=0)

    kmean = ksum_ref[0] * (1.0 / BLOCK)
    km_hi = kmean.astype(BF16)
    km_lo = (kmean - km_hi.astype(F32)).astype(BF16)
    g2 = _dot_nt(jnp.concatenate([km_hi, km_lo], axis=0), q2)
    g = g2[:n_blocks] + g2[n_blocks:]
    blk = lax.broadcasted_iota(jnp.int32, (n_blocks, nq), 0)
    valid = blk < qb
    g = jnp.where(valid, g, -jnp.inf)
    rank = jnp.zeros(g.shape, jnp.int32)
    for j in range(n_blocks):
        gj = g[j:j + 1, :]
        beats = (gj > g) | ((gj == g) & (blk > j))
        rank = rank + beats.astype(jnp.int32)
    sel = valid & (rank < TOP_K)
    pen_ref[0:n_blocks] = jnp.where(sel, 0.0, NEG_BIG)
    pen_ref[n_blocks:2 * n_blocks] = jnp.zeros((n_blocks, nq), F32)

    def item(j):
        own = j == 0
        kb = jnp.where(own, qb, j - 1)
        tile = jnp.where(own, 0, jnp.where(j == qb, 1, 2))
        pen_row = jnp.where(own, n_blocks, j - 1)
        return kb, tile, pen_row

    def item_scores(j):
        kb, tile, pen_row = item(j)
        kblk = k_ref[0, pl.ds(pl.multiple_of(kb * BLOCK, BLOCK), BLOCK), :]
        return _dot_nt(kblk, q2) + pen_ref[pl.ds(pen_row, 1), :] + bias_ref[0, tile]

    def pv(kb, p):
        vt = vt_ref[0, kb]
        return jnp.concatenate([_dot(vt[:HEAD_DIM], p[:, :BLOCK]), _dot(vt[HEAD_DIM:], p[:, BLOCK:])], axis=1)

    n_items = qb + 1
    s_ref[...] = item_scores(0)
    m_ref[...] = jnp.full(m_ref.shape, NEG_BIG, F32)
    l_ref[...] = jnp.zeros(l_ref.shape, F32)
    alpha_ref[...] = jnp.ones(alpha_ref.shape, F32)
    acc_ref[...] = jnp.zeros(acc_ref.shape, F32)
    p_ref[...] = jnp.zeros(p_ref.shape, BF16)

    def trip(j, carry):
        s = s_ref[...]
        s_next = item_scores(jnp.minimum(j + 1, n_items - 1))
        kb_prev, _, _ = item(jnp.maximum(j - 1, 0))
        acc = alpha_ref[...] * acc_ref[...] + pv(kb_prev, p_ref[...])
        m_old = m_ref[...]
        m_new = jnp.maximum(m_old, jnp.max(s, axis=0, keepdims=True))
        alpha = jnp.exp2(m_old - m_new)
        p = jnp.exp2(s - m_new)
        l_ref[...] = alpha * l_ref[...] + jnp.sum(p, axis=0, keepdims=True)
        m_ref[...] = m_new
        acc_ref[...] = acc
        alpha_ref[...] = alpha
        p_ref[...] = p.astype(BF16)
        s_ref[...] = s_next
        return carry

    lax.fori_loop(0, n_items, trip, 0)
    kb_last, _, _ = item(qb)
    acc = alpha_ref[...] * acc_ref[...] + pv(kb_last, p_ref[...])
    out = acc / l_ref[...]
    out_t = jnp.concatenate([out[:, :BLOCK], out[:, BLOCK:]], axis=0)
    o_ref[0] = out_t.T.astype(o_ref.dtype)

    for i in range(pages_at_head, pages_per_step):
        sc_ref[i] = _page_scores(q2s, pbuf[slot, i])


def _prompt_attention(q, k, vt_blocks, ksum, bias_tiles, page_table_flat, q_sample, cache_kt, n_pages):
    b, s, _ = q.shape
    n_blocks = s // BLOCK
    n_hp = N_HEADS // 2
    n_steps = b * n_hp * n_blocks
    n_seq = q_sample.shape[0]
    total_pages = n_seq * n_pages
    pages_per_step = total_pages // n_steps
    assert pages_per_step * n_steps == total_pages and n_pages % pages_per_step == 0

    def step_of(bi, hp, qb):
        return (bi * n_hp + hp) * n_blocks + qb

    attn, scores = pl.pallas_call(
        functools.partial(_attn_kernel, pages_per_step=pages_per_step),
        grid_spec=pltpu.PrefetchScalarGridSpec(
            num_scalar_prefetch=1,
            grid=(b, n_hp, n_blocks),
            in_specs=[
                pl.BlockSpec((1, BLOCK, 128), lambda bi, hp, qb, pt: (bi, qb, hp)),
                pl.BlockSpec((1, s, 128), lambda bi, hp, qb, pt: (bi, 0, hp)),
                pl.BlockSpec((1, n_blocks, 128, BLOCK), lambda bi, hp, qb, pt: (bi, 0, hp, 0)),
                pl.BlockSpec((1, n_blocks, 128), lambda bi, hp, qb, pt: (bi, 0, hp)),
                pl.BlockSpec((1, 3, BLOCK, 2 * BLOCK), lambda bi, hp, qb, pt: (hp, 0, 0, 0)),
                pl.BlockSpec((1, 1, ATTN_W),
                             lambda bi, hp, qb, pt: (step_of(bi, hp, qb) * pages_per_step // n_pages, 0, 0)),
                pl.BlockSpec(memory_space=pl.ANY),
            ],
            out_specs=[
                pl.BlockSpec((1, BLOCK, 128), lambda bi, hp, qb, pt: (bi, qb, hp)),
                pl.BlockSpec((pages_per_step, N_HEADS, PAGE), lambda bi, hp, qb, pt: (step_of(bi, hp, qb), 0, 0)),
            ],
            scratch_shapes=[
                pltpu.VMEM((2 * n_blocks, 2 * BLOCK), F32),
                pltpu.VMEM((1, 2 * BLOCK), F32),
                pltpu.VMEM((1, 2 * BLOCK), F32),
                pltpu.VMEM((1, 2 * BLOCK), F32),
                pltpu.VMEM((HEAD_DIM, 2 * BLOCK), F32),
                pltpu.VMEM((BLOCK, 2 * BLOCK), F32),
                pltpu.VMEM((BLOCK, 2 * BLOCK), BF16),
                pltpu.VMEM((STREAM_SLOTS, pages_per_step, N_HEADS, HEAD_DIM, PAGE), F32),
                pltpu.SemaphoreType.DMA((STREAM_SLOTS,)),
            ],
        ),
        out_shape=[jax.ShapeDtypeStruct((b, s, ATTN_W), BF16),
                   jax.ShapeDtypeStruct((total_pages, N_HEADS, PAGE), F32)],
        compiler_params=pltpu.CompilerParams(
            dimension_semantics=("arbitrary", "arbitrary", "arbitrary"),
            vmem_limit_bytes=VMEM_LIMIT),
    )(page_table_flat, q, k, vt_blocks, ksum, bias_tiles, q_sample, cache_kt)
    return attn, scores.reshape(n_seq, n_pages, N_HEADS, PAGE)


def _merge_ffn_kernel(*refs, tm, sample, tiles_per_seq):
    if sample:
        (x_ref, at_ref, sin_ref, ga_ref, gb_ref, wa_ref, wc_ref, wo_ref, n2_ref, wup_ref, fcw_ref,
         wdn_ref, fs0_ref, fs1_ref, y_ref, g_ref, acc_ref) = refs
    else:
        (x_ref, at_ref, sin_ref, ga_ref, gb_ref, wa_ref, wc_ref, wo_ref, n2_ref, wup_ref, fcw_ref,
         wdn_ref, y_ref, fst_ref, acc_ref, gcar_ref) = refs

    a = _dot(at_ref[...].astype(BF16), wa_ref[...])
    b = _dot(sin_ref[...], wc_ref[...])
    merged = jax.nn.sigmoid(ga_ref[...].astype(F32)) * a + jax.nn.sigmoid(gb_ref[...].astype(F32)) * b
    x1 = x_ref[...] + _dot(merged.astype(BF16), wo_ref[...])
    ms = jnp.mean(x1 * x1, axis=-1, keepdims=True)
    xn2 = (x1 * lax.rsqrt(ms + EPS) * n2_ref[...]).astype(BF16)
    acc_ref[...] = x1

    if not sample:
        @pl.when(pl.program_id(0) % tiles_per_seq == 0)
        def _():
            gcar_ref[...] = jnp.zeros_like(gcar_ref)

    for c in range(N_FF_CHUNKS):
        cols = slice(c * FF_CHUNK, (c + 1) * FF_CHUNK)
        g = _dot(xn2, wup_ref[:, cols])
        u = _dot(xn2, wup_ref[:, D_FF + c * FF_CHUNK:D_FF + (c + 1) * FF_CHUNK])
        w0, w1, w2 = fcw_ref[0:1, cols], fcw_ref[1:2, cols], fcw_ref[2:3, cols]
        if sample:
            gc = fs0_ref[:, cols] * w0 + fs1_ref[:, cols] * w1 + g * w2
            g_ref[:, cols] = g
        else:
            ge = jnp.concatenate([gcar_ref[:, cols], g], axis=0)
            gc = ge[6:6 + tm] * w0 + ge[7:7 + tm] * w1 + g * w2
            gcar_ref[:, cols] = g[tm - 8:tm]
            fst_ref[0, :, cols] = g[tm - 2:tm]
        hidden = (gc * jax.nn.sigmoid(gc) * u).astype(BF16)
        acc_ref[...] += _dot(hidden, wdn_ref[cols, :])
    y_ref[...] = acc_ref[...]


def _merge_ffn(x, attn, s_in, ga, gb, wa, wc, wo, n2, wup, fcw, wdn, states, *, sample, seq_len=1):
    n_tok = x.shape[0]
    tm = n_tok if sample else TOKEN_TILE
    n_tiles = n_tok // tm
    tiles_per_seq = 1 if sample else seq_len // tm

    def row(width):
        return pl.BlockSpec((tm, width), lambda i: (i, 0))

    in_specs = [row(D_MODEL), row(ATTN_W), row(CONV_CH), row(D_MODEL), row(D_MODEL),
                _const_spec((ATTN_W, D_MODEL)), _const_spec((CONV_CH, D_MODEL)),
                _const_spec((D_MODEL, D_MODEL)), _const_spec((1, D_MODEL)),
                _const_spec((D_MODEL, 2 * D_FF)), _const_spec((3, D_FF)), _const_spec((D_FF, D_MODEL))]
    args = [x, attn, s_in, ga, gb, wa, wc, wo, n2, wup, fcw, wdn]
    scratch = [pltpu.VMEM((tm, D_MODEL), F32)]
    if sample:
        in_specs += [row(D_FF), row(D_FF)]
        args += list(states)
        out_specs = [row(D_MODEL), row(D_FF)]
        out_shape = [jax.ShapeDtypeStruct((n_tok, D_MODEL), F32),
                     jax.ShapeDtypeStruct((n_tok, D_FF), F32)]
    else:
        n_seq = n_tiles // tiles_per_seq
        out_specs = [row(D_MODEL), pl.BlockSpec((1, 2, D_FF), lambda i: (i // tiles_per_seq, 0, 0))]
        out_shape = [jax.ShapeDtypeStruct((n_tok, D_MODEL), F32),
                     jax.ShapeDtypeStruct((n_seq, 2, D_FF), F32)]
        scratch.append(pltpu.VMEM((8, D_FF), F32))
    return pl.pallas_call(
        functools.partial(_merge_ffn_kernel, tm=tm, sample=sample, tiles_per_seq=tiles_per_seq),
        grid=(n_tiles,),
        in_specs=in_specs,
        out_specs=out_specs,
        out_shape=out_shape,
        scratch_shapes=scratch,
        compiler_params=pltpu.CompilerParams(dimension_semantics=("arbitrary",),
                                             vmem_limit_bytes=VMEM_LIMIT),
    )(*args)


SEQ_PER_SELECT_STEP = 8


def _select_kernel(sc_ref, idx_ref):
    n_pages = sc_ref.shape[1]
    pages_per_block = BLOCK // PAGE
    n_blocks = n_pages // pages_per_block
    blk = lax.broadcasted_iota(jnp.int32, (n_blocks, N_HEADS, 128), 0)
    for s in range(SEQ_PER_SELECT_STEP):
        page_sum = jnp.sum(sc_ref[s], axis=-1, keepdims=True)
        block_sum = jnp.sum(page_sum.reshape(n_blocks, pages_per_block, N_HEADS, 1), axis=1)
        gate = jnp.broadcast_to(block_sum * (1.0 / BLOCK), (n_blocks, N_HEADS, 128))
        for r in range(TOP_K):
            best = jnp.max(gate, axis=0, keepdims=True)
            idx = jnp.min(jnp.where(gate == best, blk, n_blocks), axis=0, keepdims=True)
            idx_ref[s, r] = idx[0]
            gate = jnp.where(blk == idx, -jnp.inf, gate)


def _select_blocks(scores):
    n_seq, n_pages = scores.shape[:2]
    return pl.pallas_call(
        _select_kernel,
        grid=(n_seq // SEQ_PER_SELECT_STEP,),
        in_specs=[pl.BlockSpec((SEQ_PER_SELECT_STEP, n_pages, N_HEADS, PAGE), lambda i: (i, 0, 0, 0))],
        out_specs=pl.BlockSpec((SEQ_PER_SELECT_STEP, TOP_K, N_HEADS, 128), lambda i: (i, 0, 0, 0)),
        out_shape=jax.ShapeDtypeStruct((n_seq, TOP_K, N_HEADS, 128), jnp.int32),
    )(scores)


PAGES_PER_BLOCK = BLOCK // PAGE
CHUNKS_PER_HEAD = TOP_K * PAGES_PER_BLOCK
N_CHUNKS = N_HEADS * CHUNKS_PER_HEAD
KEYS_PER_HEAD = CHUNKS_PER_HEAD * PAGE


def _sample_attn_kernel(pt_ref, idx_ref, sc_ref, q_ref, kn_ref, vn_ref, brow_ref, ownb_ref, cv_ref,
                        o_ref, vbuf, logit_ref, sem, *, n_seq, n_pages):
    b = pl.program_id(0)
    slot = b % 2

    def picked_page(seq, h, r, j):
        n = idx_ref[(seq * TOP_K + r) * N_HEADS + h]
        return n, n * PAGES_PER_BLOCK + j

    def copies(seq, slot_):
        out = []
        for h in range(N_HEADS):
            for r in range(TOP_K):
                for j in range(PAGES_PER_BLOCK):
                    _, logical = picked_page(seq, h, r, j)
                    page = pt_ref[seq * n_pages + logical]
                    c = (h * TOP_K + r) * PAGES_PER_BLOCK + j
                    out.append(pltpu.make_async_copy(cv_ref.at[0, page, h], vbuf.at[slot_, c], sem.at[slot_]))
        return out

    @pl.when(b == 0)
    def _():
        for c in copies(0, 0):
            c.start()

    @pl.when(b + 1 < n_seq)
    def _():
        for c in copies(b + 1, 1 - slot):
            c.start()

    last_block = n_pages // PAGES_PER_BLOCK - 1
    for h in range(N_HEADS):
        for r in range(TOP_K):
            for j in range(PAGES_PER_BLOCK):
                n, logical = picked_page(b, h, r, j)
                piece = sc_ref[0, logical][h:h + 1, :]
                near = brow_ref[h][:, j * PAGE:(j + 1) * PAGE]
                piece = piece + jnp.where(n == last_block, near, jnp.zeros_like(near))
                c = r * PAGES_PER_BLOCK + j
                logit_ref[h:h + 1, c * PAGE:(c + 1) * PAGE] = piece
    logits = logit_ref[...]

    q = q_ref[0]
    own = jnp.sum(q * kn_ref[0], axis=-1, keepdims=True) + ownb_ref[:, 0, 0:1]
    m = jnp.maximum(jnp.max(logits, axis=-1, keepdims=True), own)
    p = jnp.exp(logits - m)
    p_own = jnp.exp(own - m)
    denom = jnp.sum(p, axis=-1, keepdims=True) + p_own

    p_wide = jnp.concatenate([p] * N_HEADS, axis=1)
    col = lax.broadcasted_iota(jnp.int32, p_wide.shape, 1)
    row = lax.broadcasted_iota(jnp.int32, p_wide.shape, 0)
    p_wide = jnp.where(col // KEYS_PER_HEAD == row, p_wide, 0.0)
    p16 = jnp.concatenate([p_wide, jnp.zeros_like(p_wide)], axis=0).astype(BF16)

    pltpu.make_async_copy(cv_ref.at[0, pl.ds(0, N_CHUNKS), 0], vbuf.at[slot], sem.at[slot]).wait()
    vt = jnp.concatenate([vbuf[slot, c] for c in range(N_CHUNKS)], axis=1).astype(BF16)
    ctx = _dot_nt(p16, vt)[:N_HEADS] + p_own * vn_ref[0]
    o_ref[0] = ctx / denom


def _sample_attention(page_table_flat, idx_flat, scores, q_heads, k_new, v_new, brow, ownb, cache_vt, n_pages):
    n_seq = q_heads.shape[0]
    head_spec = pl.BlockSpec((1, N_HEADS, HEAD_DIM), lambda b, pt, ix: (b, 0, 0))
    return pl.pallas_call(
        functools.partial(_sample_attn_kernel, n_seq=n_seq, n_pages=n_pages),
        grid_spec=pltpu.PrefetchScalarGridSpec(
            num_scalar_prefetch=2,
            grid=(n_seq,),
            in_specs=[pl.BlockSpec((1, n_pages, N_HEADS, PAGE), lambda b, pt, ix: (b, 0, 0, 0)),
                      head_spec, head_spec, head_spec,
                      pl.BlockSpec((N_HEADS, 1, BLOCK), lambda b, pt, ix: (0, 0, 0)),
                      pl.BlockSpec((N_HEADS, 1, 128), lambda b, pt, ix: (0, 0, 0)),
                      pl.BlockSpec(memory_space=pl.ANY)],
            out_specs=head_spec,
            scratch_shapes=[pltpu.VMEM((2, N_CHUNKS, HEAD_DIM, PAGE), F32),
                            pltpu.VMEM((N_HEADS, KEYS_PER_HEAD), F32),
                            pltpu.SemaphoreType.DMA((2,))],
        ),
        out_shape=jax.ShapeDtypeStruct((n_seq, N_HEADS, HEAD_DIM), F32),
        compiler_params=pltpu.CompilerParams(dimension_semantics=("arbitrary",),
                                             vmem_limit_bytes=VMEM_LIMIT),
    )(page_table_flat, idx_flat, scores, q_heads, k_new, v_new, brow, ownb, cache_vt)


def kernel(x_prompt, x_sample, cache_k, cache_v, page_table, state_conv, state_ffn, norm1_w, w_in, q_norm_w, k_norm_w, conv_w, w_attn_up, w_conv_out, w_o, norm2_w, w_ffn_up, ffn_conv_w, w_ffn_down, rel_bias):
    batch, seq, _ = x_prompt.shape
    n_seq = x_sample.shape[0]
    n_pages = page_table.shape[1]

    w_in_b = w_in[0].astype(BF16)
    wa = w_attn_up[0].astype(BF16)
    wc = w_conv_out[0].astype(BF16)
    wo = w_o[0].astype(BF16)
    wup = w_ffn_up[0].astype(BF16)
    wdn = w_ffn_down[0].astype(BF16)
    fcw = ffn_conv_w[0]
    n1 = norm1_w[0].reshape(1, D_MODEL)
    n2 = norm2_w[0].reshape(1, D_MODEL)
    qn = jnp.tile(q_norm_w[0], N_HEADS).reshape(1, ATTN_W)
    kn = jnp.tile(k_norm_w[0], N_HEADS).reshape(1, ATTN_W)
    lane = np.arange(BLOCK)
    bd = jnp.asarray((lane[:, None] // HEAD_DIM == lane[None, :] // HEAD_DIM) / HEAD_DIM, BF16)
    cw = conv_w[0]

    bias_tiles, bias_row, bias_own = _bias_tables(rel_bias)

    xp = x_prompt.reshape(batch * seq, D_MODEL)
    (q_b, kt_p, vt_p, k_b, vt_b, ksum_p, sin_p, ga_p, gb_p, conv_p) = _inproj(
        xp, n1, w_in_b, qn, kn, bd, cw, None, sample=False, seq_len=seq)
    xs = x_sample.reshape(n_seq, D_MODEL)
    (q_s, k_s, v_s, sin_s, ga_s, gb_s, u_s) = _inproj(
        xs, n1, w_in_b, qn, kn, bd, cw, (state_conv[0, :, 0, :], state_conv[0, :, 1, :]), sample=True)
    pt_flat = page_table.reshape(-1)
    cache_kt = cache_k.transpose(0, 1, 3, 4, 2)
    cache_vt = cache_v.transpose(0, 1, 3, 4, 2)

    attn_p, scores = _prompt_attention(q_b.reshape(batch, seq, ATTN_W), k_b.reshape(batch, seq, ATTN_W), vt_b,
                                       ksum_p.reshape(batch, seq // BLOCK, ATTN_W), bias_tiles,
                                       pt_flat, q_s.reshape(n_seq, 1, ATTN_W), cache_kt, n_pages)
    y_p, ffn_p = _merge_ffn(xp, attn_p.reshape(batch * seq, ATTN_W), sin_p, ga_p, gb_p,
                            wa, wc, wo, n2, wup, fcw, wdn, None, sample=False, seq_len=seq)

    idx = _select_blocks(scores)[..., 0]
    attn_s = _sample_attention(pt_flat, idx.reshape(-1), scores, q_s.reshape(n_seq, N_HEADS, HEAD_DIM),
                               k_s.reshape(n_seq, N_HEADS, HEAD_DIM), v_s.reshape(n_seq, N_HEADS, HEAD_DIM),
                               bias_row, bias_own, cache_vt, n_pages)

    y_s, g_s = _merge_ffn(xs, attn_s.reshape(n_seq, ATTN_W), sin_s, ga_s, gb_s,
                          wa, wc, wo, n2, wup, fcw, wdn,
                          (state_ffn[0, :, 0, :], state_ffn[0, :, 1, :]), sample=True)

    return (
        y_p.reshape(batch, seq, D_MODEL),
        y_s.reshape(n_seq, 1, D_MODEL),
        kt_p.reshape(1, batch, N_HEADS, HEAD_DIM, seq).transpose(0, 1, 4, 2, 3),
        vt_p.reshape(1, batch, N_HEADS, HEAD_DIM, seq).transpose(0, 1, 4, 2, 3),
        conv_p[None],
        ffn_p[None],
        k_s.reshape(1, n_seq, 1, N_HEADS, HEAD_DIM),
        v_s.reshape(1, n_seq, 1, N_HEADS, HEAD_DIM),
        jnp.stack([state_conv[0, :, 1, :], u_s], axis=1)[None],
        jnp.stack([state_ffn[0, :, 1, :], g_s], axis=1)[None],
    )
```

```python
import functools
import math

import numpy as np
import jax
import jax.numpy as jnp
from jax import lax
from jax.experimental import pallas as pl
from jax.experimental.pallas import tpu as pltpu

F32 = jnp.float32
BF16 = jnp.bfloat16

D_MODEL = 1024
N_HEADS = 8
HEAD_DIM = 64
ATTN_W = N_HEADS * HEAD_DIM
CONV_CH = D_MODEL // 2
D_FF = 2816
BLOCK = 256
TOP_K = 3
PAGE = 128
NUM_BUCKETS = 32
MAX_DISTANCE = 128
EPS = 1e-6
N_COLS = 3 * ATTN_W + 3 * CONV_CH + 2 * D_MODEL

FF_CHUNK = 256
N_FF_CHUNKS = D_FF // FF_CHUNK
TOKEN_TILE = 512
NEG_BIG = -1e30
VMEM_LIMIT = 60 * 1024 * 1024
LOG2E = math.log2(math.e)


def _bucket_thresholds():
    max_exact = NUM_BUCKETS // 2
    d = np.arange(0, 4 * MAX_DISTANCE)
    logd = np.log(np.maximum(d, 1) / max_exact)
    large = max_exact + (logd / math.log(MAX_DISTANCE / max_exact) * (NUM_BUCKETS - max_exact)).astype(np.int32)
    bucket = np.where(d < max_exact, d, np.minimum(large, NUM_BUCKETS - 1))
    return [int(np.argmax(bucket >= b)) for b in range(NUM_BUCKETS)]


BUCKET_THR = _bucket_thresholds()


def _dot(a, b):
    return jnp.dot(a, b, preferred_element_type=F32)


def _dot_nt(a, b):
    return lax.dot_general(a, b, (((1,), (1,)), ((), ())), preferred_element_type=F32)


def _sigmoid(x):
    return 0.5 * jnp.tanh(0.5 * x) + 0.5


def _const_spec(shape):
    nd = len(shape)
    return pl.BlockSpec(shape, lambda *_: (0,) * nd, pipeline_mode=pl.Buffered(1))


def _bias_from_distance(d, rb_ref, h, b_far):
    val = jnp.zeros(d.shape, F32)
    for b in range(NUM_BUCKETS - 2, -1, -1):
        val = jnp.where(d < BUCKET_THR[b + 1], rb_ref[b, h] - b_far, val)
    return val


def _bias_kernel(rb_ref, tile_ref, row_ref, own_ref):
    hp = pl.program_id(0)
    j = lax.broadcasted_iota(jnp.int32, (BLOCK, BLOCK), 0)
    i = lax.broadcasted_iota(jnp.int32, (BLOCK, BLOCK), 1)
    jr = lax.broadcasted_iota(jnp.int32, (1, BLOCK), 1)
    for e in range(2):
        h = 2 * hp + e
        b_far = rb_ref[NUM_BUCKETS - 1, h]
        lanes = slice(e * BLOCK, (e + 1) * BLOCK)
        tile_ref[0, 0, :, lanes] = jnp.where(j <= i, _bias_from_distance(i - j, rb_ref, h, b_far) * LOG2E, NEG_BIG)
        tile_ref[0, 1, :, lanes] = _bias_from_distance(i - j + BLOCK, rb_ref, h, b_far) * LOG2E
        row_ref[e] = _bias_from_distance(BLOCK - jr, rb_ref, h, b_far)
        own_ref[e] = jnp.zeros((1, 128), F32) + (rb_ref[0, h] - b_far)
    tile_ref[0, 2] = jnp.zeros((BLOCK, 2 * BLOCK), F32)


def _bias_tables(rel_bias):
    n_hp = N_HEADS // 2
    return pl.pallas_call(
        _bias_kernel,
        grid=(n_hp,),
        in_specs=[pl.BlockSpec(memory_space=pltpu.SMEM)],
        out_specs=[
            pl.BlockSpec((1, 3, BLOCK, 2 * BLOCK), lambda hp: (hp, 0, 0, 0)),
            pl.BlockSpec((2, 1, BLOCK), lambda hp: (hp, 0, 0)),
            pl.BlockSpec((2, 1, 128), lambda hp: (hp, 0, 0)),
        ],
        out_shape=[
            jax.ShapeDtypeStruct((n_hp, 3, BLOCK, 2 * BLOCK), F32),
            jax.ShapeDtypeStruct((N_HEADS, 1, BLOCK), F32),
            jax.ShapeDtypeStruct((N_HEADS, 1, 128), F32),
        ],
    )(rel_bias)


def _inproj_kernel(*refs, tm, sample, tiles_per_seq, stream):
    if sample:
        (x_ref, n1_ref, w_ref, qn_ref, kn_ref, bd_ref, cw_ref, st0_ref, st1_ref,
         q_ref, k_ref, v_ref, sin_ref, ga_ref, gb_ref, u_ref) = refs
    else:
        (pt_ref, x_ref, n1_ref, w_ref, qn_ref, kn_ref, bd_ref, cw_ref, qs_ref, ck_ref,
         q_ref, kt_ref, vt_ref, kb_ref, vtb_ref, ksum_ref, sin_ref, ga_ref, gb_ref, cst_ref, sc_ref,
         ucar_ref, pbuf, psem) = refs
        first_page, pages_per_step = stream
        page0 = first_page + pl.program_id(0) * pages_per_step
        for i in range(pages_per_step):
            pltpu.make_async_copy(ck_ref.at[0, pt_ref[page0 + i]], pbuf.at[i], psem.at[0]).start()

    x = x_ref[...]
    ms = jnp.mean(x * x, axis=-1, keepdims=True)
    xn = (x * lax.rsqrt(ms + EPS) * n1_ref[...]).astype(BF16)

    def proj(a, b):
        return _dot(xn, w_ref[:, a:b])

    def head_norm(t, w_row):
        sq = (t * t).astype(BF16)
        bd = bd_ref[...]
        half = ATTN_W // 2
        msq = jnp.concatenate([_dot(sq[:, :half], bd), _dot(sq[:, half:], bd)], axis=1)
        return t * lax.rsqrt(msq + EPS) * w_row

    q = head_norm(proj(0, ATTN_W), qn_ref[...]) * (HEAD_DIM ** -0.5)
    k = head_norm(proj(ATTN_W, 2 * ATTN_W), kn_ref[...])
    v = proj(2 * ATTN_W, 3 * ATTN_W)
    c0 = 3 * ATTN_W
    cb = proj(c0, c0 + CONV_CH)
    u = proj(c0 + CONV_CH, c0 + 2 * CONV_CH) * proj(c0 + 2 * CONV_CH, c0 + 3 * CONV_CH)
    g0 = c0 + 3 * CONV_CH
    ga_ref[...] = proj(g0, g0 + D_MODEL).astype(ga_ref.dtype)
    gb_ref[...] = proj(g0 + D_MODEL, g0 + 2 * D_MODEL).astype(gb_ref.dtype)

    w0, w1, w2 = cw_ref[0:1, :], cw_ref[1:2, :], cw_ref[2:3, :]
    if sample:
        q_ref[...] = q
        k_ref[...] = k
        v_ref[...] = v
        uc = st0_ref[...] * w0 + st1_ref[...] * w1 + u * w2
        u_ref[...] = u
    else:
        q_ref[...] = (q * LOG2E).astype(BF16)
        kb_ref[...] = k.astype(BF16)
        kt_ref[0] = k.T
        v_t = v.T
        vt_ref[0] = v_t
        for r in range(tm // BLOCK):
            vtb_ref[0, r] = v_t[:, r * BLOCK:(r + 1) * BLOCK].astype(BF16)
            ksum_ref[0, r:r + 1, :] = jnp.sum(k[r * BLOCK:(r + 1) * BLOCK], axis=0, keepdims=True)

        @pl.when(pl.program_id(0) % tiles_per_seq == 0)
        def _():
            ucar_ref[...] = jnp.zeros_like(ucar_ref)

        ue = jnp.concatenate([ucar_ref[...], u], axis=0)
        uc = ue[6:6 + tm] * w0 + ue[7:7 + tm] * w1 + u * w2
        ucar_ref[...] = u[tm - 8:tm]
        cst_ref[0] = u[tm - 2:tm]
    sin_ref[...] = (cb * uc).astype(sin_ref.dtype)

    if not sample:
        pltpu.make_async_copy(ck_ref.at[0, pl.ds(0, pages_per_step)], pbuf, psem.at[0]).wait()
        q2s = _block_diag_query(qs_ref[0])
        for i in range(pages_per_step):
            sc_ref[i] = _page_scores(q2s, pbuf[i])


def _inproj(x, n1, w_in, qn, kn, bd, conv_w, states, *, sample, seq_len=1, stream=None):
    n_tok = x.shape[0]
    tm = n_tok if sample else TOKEN_TILE
    n_tiles = n_tok // tm
    tiles_per_seq = 1 if sample else seq_len // tm

    def row(width):
        return pl.BlockSpec((tm, width), lambda i, *_: (i, 0))

    in_specs = [row(D_MODEL), _const_spec((1, D_MODEL)), _const_spec((D_MODEL, N_COLS)),
                _const_spec((1, ATTN_W)), _const_spec((1, ATTN_W)), _const_spec((BLOCK, BLOCK)),
                _const_spec((3, CONV_CH))]
    args = [x, n1, w_in, qn, kn, bd, conv_w]
    if sample:
        in_specs += [row(CONV_CH), row(CONV_CH)]
        args += list(states)
        out_specs = [row(ATTN_W), row(ATTN_W), row(ATTN_W), row(CONV_CH), row(D_MODEL), row(D_MODEL),
                     row(CONV_CH)]
        out_shape = [jax.ShapeDtypeStruct((n_tok, ATTN_W), F32)] * 3 + [
            jax.ShapeDtypeStruct((n_tok, CONV_CH), BF16),
            jax.ShapeDtypeStruct((n_tok, D_MODEL), BF16),
            jax.ShapeDtypeStruct((n_tok, D_MODEL), BF16),
            jax.ShapeDtypeStruct((n_tok, CONV_CH), F32)]
        scratch = []
    else:
        blocks_per_tile = tm // BLOCK
        n_seq = n_tiles // tiles_per_seq
        page_table_flat, q_sample, cache_kt, n_pages, first_page = stream
        pps = INPROJ_STREAM_PAGES
        total_pages = q_sample.shape[0] * n_pages
        assert first_page % pps == 0 and n_pages % pps == 0 and first_page + n_tiles * pps == total_pages
        assert first_page % n_pages == 0
        in_specs += [pl.BlockSpec((1, 1, ATTN_W), lambda i, *_: ((first_page + i * pps) // n_pages, 0, 0)),
                     pl.BlockSpec(memory_space=pl.ANY)]
        args = [page_table_flat] + args + [q_sample, cache_kt]
        t_spec = pl.BlockSpec((1, ATTN_W, tm), lambda i, *_: (i // tiles_per_seq, 0, i % tiles_per_seq))
        out_specs = [row(ATTN_W), t_spec, t_spec, row(ATTN_W),
                     pl.BlockSpec((1, blocks_per_tile, ATTN_W, BLOCK),
                                  lambda i, *_: (i // tiles_per_seq, i % tiles_per_seq, 0, 0)),
                     pl.BlockSpec((1, blocks_per_tile, ATTN_W), lambda i, *_: (i, 0, 0)),
                     row(CONV_CH), row(D_MODEL), row(D_MODEL),
                     pl.BlockSpec((1, 2, CONV_CH), lambda i, *_: (i // tiles_per_seq, 0, 0)),
                     pl.BlockSpec((pps, N_HEADS, PAGE), lambda i, *_: (i, 0, 0))]
        out_shape = [jax.ShapeDtypeStruct((n_tok, ATTN_W), BF16),
                     jax.ShapeDtypeStruct((n_seq, ATTN_W, seq_len), F32),
                     jax.ShapeDtypeStruct((n_seq, ATTN_W, seq_len), F32),
                     jax.ShapeDtypeStruct((n_tok, ATTN_W), BF16),
                     jax.ShapeDtypeStruct((n_seq, seq_len // BLOCK, ATTN_W, BLOCK), BF16),
                     jax.ShapeDtypeStruct((n_tiles, blocks_per_tile, ATTN_W), F32),
                     jax.ShapeDtypeStruct((n_tok, CONV_CH), BF16),
                     jax.ShapeDtypeStruct((n_tok, D_MODEL), BF16),
                     jax.ShapeDtypeStruct((n_tok, D_MODEL), BF16),
                     jax.ShapeDtypeStruct((n_tiles // tiles_per_seq, 2, CONV_CH), F32),
                     jax.ShapeDtypeStruct((n_tiles * pps, N_HEADS, PAGE), F32)]
        scratch = [pltpu.VMEM((8, CONV_CH), F32),
                   pltpu.VMEM((pps, N_HEADS, HEAD_DIM, PAGE), F32),
                   pltpu.SemaphoreType.DMA((1,))]
    return pl.pallas_call(
        functools.partial(_inproj_kernel, tm=tm, sample=sample, tiles_per_seq=tiles_per_seq,
                          stream=None if sample else (stream[4], INPROJ_STREAM_PAGES)),
        grid_spec=pltpu.PrefetchScalarGridSpec(
            num_scalar_prefetch=0 if sample else 1,
            grid=(n_tiles,),
            in_specs=in_specs,
            out_specs=out_specs,
            scratch_shapes=scratch,
        ),
        out_shape=out_shape,
        compiler_params=pltpu.CompilerParams(dimension_semantics=("arbitrary",),
                                             vmem_limit_bytes=VMEM_LIMIT),
    )(*args)


STREAM_SLOTS = 3
INPROJ_STREAM_PAGES = 32
FFN_STREAM_PAGES = 32


def _block_diag_query(q_row):
    row = lax.broadcasted_iota(jnp.int32, (N_HEADS, ATTN_W), 0)
    lane_head = lax.broadcasted_iota(jnp.int32, (N_HEADS, ATTN_W), 1) // HEAD_DIM
    q_rows = jnp.where(row == lane_head, q_row, 0.0)
    q_hi = q_rows.astype(BF16).astype(F32)
    return jnp.concatenate([q_hi, q_rows - q_hi], axis=0).astype(BF16)


def _page_scores(q2, page):
    s2 = _dot(q2, page.reshape(ATTN_W, PAGE).astype(BF16))
    return s2[:N_HEADS] + s2[N_HEADS:]


def _attn_kernel(pt_ref, q_ref, k_ref, vt_ref, ksum_ref, bias_ref, qs_ref, ck_ref, o_ref, sc_ref,
                 pen_ref, m_ref, l_ref, alpha_ref, acc_ref, s_ref, p_ref, pbuf, psem, *, pages_per_step, n_pages):
    qb = pl.program_id(2)
    n_blocks = k_ref.shape[1] // BLOCK
    nq = 2 * BLOCK
    step = (pl.program_id(0) * pl.num_programs(1) + pl.program_id(1)) * pl.num_programs(2) + qb
    n_steps = pl.num_programs(0) * pl.num_programs(1) * pl.num_programs(2)

    def page_copies(step_, slot_):
        return [pltpu.make_async_copy(ck_ref.at[0, pt_ref[step_ * pages_per_step + i]], pbuf.at[slot_, i],
                                      psem.at[slot_]) for i in range(pages_per_step)]

    @pl.when(step == 0)
    def _():
        for ahead in range(STREAM_SLOTS - 1):
            for c in page_copies(ahead, ahead):
                c.start()

    @pl.when(step + (STREAM_SLOTS - 1) < n_steps)
    def _():
        nxt = step + (STREAM_SLOTS - 1)
        for c in page_copies(nxt, nxt % STREAM_SLOTS):
            c.start()

    slot = step % STREAM_SLOTS
    pltpu.make_async_copy(ck_ref.at[0, pl.ds(0, pages_per_step)], pbuf.at[slot], psem.at[slot]).wait()
    page0 = step * pages_per_step
    seq0 = page0 // n_pages
    seq1 = jnp.minimum(seq0 + 1, qs_ref.shape[0] - 1)
    q2s0 = _block_diag_query(qs_ref[pl.ds(seq0, 1), :])
    q2s1 = _block_diag_query(qs_ref[pl.ds(seq1, 1), :])

    def score_page(i):
        q2s = jnp.where((page0 + i) // n_pages == seq0, q2s0, q2s1)
        sc_ref[i] = _page_scores(q2s, pbuf[slot, i])

    pages_at_head = (pages_per_step * 5) // 8
    for i in range(pages_at_head):
        score_page(i)

    q = q_ref[0]
    lane_head = lax.broadcasted_iota(jnp.int32, q.shape, 1) // HEAD_DIM
    zero = jnp.zeros_like(q)
    q2 = jnp.concatenate([jnp.where(lane_head == 0, q, zero), jnp.where(lane_head == 1, q, zero)], axis=0)

    kmean = ksum_ref[0] * (1.0 / BLOCK)
    km_hi = kmean.astype(BF16)
    km_lo = (kmean - km_hi.astype(F32)).astype(BF16)
    g2 = _dot_nt(jnp.concatenate([km_hi, km_lo], axis=0), q2)
    g = g2[:n_blocks] + g2[n_blocks:]
    blk = lax.broadcasted_iota(jnp.int32, (n_blocks, nq), 0)
    valid = blk < qb
    g = jnp.where(valid, g, -jnp.inf)
    rank = jnp.zeros(g.shape, jnp.int32)
    for j in range(n_blocks):
        gj = g[j:j + 1, :]
        beats = (gj > g) | ((gj == g) & (blk > j))
        rank = rank + beats.astype(jnp.int32)
    sel = valid & (rank < TOP_K)
    pen_ref[0:n_blocks] = jnp.where(sel, 0.0, NEG_BIG)
    pen_ref[n_blocks:2 * n_blocks] = jnp.zeros((n_blocks, nq), F32)

    def item(j):
        own = j == 0
        kb = jnp.where(own, qb, j - 1)
        tile = jnp.where(own, 0, jnp.where(j == qb, 1, 2))
        pen_row = jnp.where(own, n_blocks, j - 1)
        return kb, tile, pen_row

    def item_scores(j):
        kb, tile, pen_row = item(j)
        kblk = k_ref[0, pl.ds(pl.multiple_of(kb * BLOCK, BLOCK), BLOCK), :]
        return _dot_nt(kblk, q2) + pen_ref[pl.ds(pen_row, 1), :] + bias_ref[0, tile]

    def pv(kb, p):
        vt = vt_ref[0, kb]
        return jnp.concatenate([_dot(vt[:HEAD_DIM], p[:, :BLOCK]), _dot(vt[HEAD_DIM:], p[:, BLOCK:])], axis=1)

    n_items = qb + 1
    s_ref[...] = item_scores(0)
    m_ref[...] = jnp.full(m_ref.shape, NEG_BIG, F32)
    l_ref[...] = jnp.zeros(l_ref.shape, F32)
    alpha_ref[...] = jnp.ones(alpha_ref.shape, F32)
    acc_ref[...] = jnp.zeros(acc_ref.shape, F32)
    p_ref[...] = jnp.zeros(p_ref.shape, BF16)

    def trip(j, carry):
        s = s_ref[...]
        s_next = item_scores(jnp.minimum(j + 1, n_items - 1))
        kb_prev, _, _ = item(jnp.maximum(j - 1, 0))
        acc = alpha_ref[...] * acc_ref[...] + pv(kb_prev, p_ref[...])
        m_old = m_ref[...]
        m_new = jnp.maximum(m_old, jnp.max(s, axis=0, keepdims=True))
        alpha = jnp.exp2(m_old - m_new)
        p = jnp.exp2(s - m_new)
        l_ref[...] = alpha * l_ref[...] + jnp.sum(p, axis=0, keepdims=True)
        m_ref[...] = m_new
        acc_ref[...] = acc
        alpha_ref[...] = alpha
        p_ref[...] = p.astype(BF16)
        s_ref[...] = s_next
        return carry

    lax.fori_loop(0, n_items, trip, 0)
    kb_last, _, _ = item(qb)
    acc = alpha_ref[...] * acc_ref[...] + pv(kb_last, p_ref[...])
    out = acc / l_ref[...]
    out_t = jnp.concatenate([out[:, :BLOCK], out[:, BLOCK:]], axis=0)
    o_ref[0] = out_t.T.astype(o_ref.dtype)

    for i in range(pages_at_head, pages_per_step):
        score_page(i)


def _prompt_attention(q, k, vt_blocks, ksum, bias_tiles, page_table_flat, q_sample, cache_kt, n_pages,
                      n_stream_pages):
    b, s, _ = q.shape
    n_blocks = s // BLOCK
    n_hp = N_HEADS // 2
    n_steps = b * n_hp * n_blocks
    n_seq = q_sample.shape[0]
    pages_per_step = n_stream_pages // n_steps
    assert pages_per_step * n_steps == n_stream_pages and pages_per_step <= n_pages

    def step_of(bi, hp, qb):
        return (bi * n_hp + hp) * n_blocks + qb

    attn, scores = pl.pallas_call(
        functools.partial(_attn_kernel, pages_per_step=pages_per_step, n_pages=n_pages),
        grid_spec=pltpu.PrefetchScalarGridSpec(
            num_scalar_prefetch=1,
            grid=(b, n_hp, n_blocks),
            in_specs=[
                pl.BlockSpec((1, BLOCK, 128), lambda bi, hp, qb, pt: (bi, qb, hp)),
                pl.BlockSpec((1, s, 128), lambda bi, hp, qb, pt: (bi, 0, hp)),
                pl.BlockSpec((1, n_blocks, 128, BLOCK), lambda bi, hp, qb, pt: (bi, 0, hp, 0)),
                pl.BlockSpec((1, n_blocks, 128), lambda bi, hp, qb, pt: (bi, 0, hp)),
                pl.BlockSpec((1, 3, BLOCK, 2 * BLOCK), lambda bi, hp, qb, pt: (hp, 0, 0, 0)),
                pl.BlockSpec((n_seq, ATTN_W), lambda bi, hp, qb, pt: (0, 0), pipeline_mode=pl.Buffered(1)),
                pl.BlockSpec(memory_space=pl.ANY),
            ],
            out_specs=[
                pl.BlockSpec((1, BLOCK, 128), lambda bi, hp, qb, pt: (bi, qb, hp)),
                pl.BlockSpec((pages_per_step, N_HEADS, PAGE), lambda bi, hp, qb, pt: (step_of(bi, hp, qb), 0, 0)),
            ],
            scratch_shapes=[
                pltpu.VMEM((2 * n_blocks, 2 * BLOCK), F32),
                pltpu.VMEM((1, 2 * BLOCK), F32),
                pltpu.VMEM((1, 2 * BLOCK), F32),
                pltpu.VMEM((1, 2 * BLOCK), F32),
                pltpu.VMEM((HEAD_DIM, 2 * BLOCK), F32),
                pltpu.VMEM((BLOCK, 2 * BLOCK), F32),
                pltpu.VMEM((BLOCK, 2 * BLOCK), BF16),
                pltpu.VMEM((STREAM_SLOTS, pages_per_step, N_HEADS, HEAD_DIM, PAGE), F32),
                pltpu.SemaphoreType.DMA((STREAM_SLOTS,)),
            ],
        ),
        out_shape=[jax.ShapeDtypeStruct((b, s, ATTN_W), BF16),
                   jax.ShapeDtypeStruct((n_stream_pages, N_HEADS, PAGE), F32)],
        compiler_params=pltpu.CompilerParams(
            dimension_semantics=("arbitrary", "arbitrary", "arbitrary"),
            vmem_limit_bytes=VMEM_LIMIT),
    )(page_table_flat, q, k, vt_blocks, ksum, bias_tiles, q_sample, cache_kt)
    return attn, scores


def _merge_ffn_kernel(*refs, tm, sample, tiles_per_seq, stream):
    if sample:
        (x_ref, at_ref, sin_ref, ga_ref, gb_ref, wa_ref, wc_ref, wo_ref, n2_ref, wup_ref, fcw_ref,
         wdn_ref, fs0_ref, fs1_ref, y_ref, g_ref, acc_ref) = refs
    else:
        (pt_ref, x_ref, at_ref, sin_ref, ga_ref, gb_ref, wa_ref, wc_ref, wo_ref, n2_ref, wup_ref, fcw_ref,
         wdn_ref, qs_ref, ck_ref, y_ref, fst_ref, sc_ref, acc_ref, gcar_ref, pbuf, psem) = refs
        first_page, pages_per_step = stream
        page0 = first_page + pl.program_id(0) * pages_per_step
        for i in range(pages_per_step):
            pltpu.make_async_copy(ck_ref.at[0, pt_ref[page0 + i]], pbuf.at[i], psem.at[0]).start()

    a = _dot(at_ref[...].astype(BF16), wa_ref[...])
    b = _dot(sin_ref[...], wc_ref[...])
    merged = _sigmoid(ga_ref[...].astype(F32)) * a + _sigmoid(gb_ref[...].astype(F32)) * b
    x1 = x_ref[...] + _dot(merged.astype(BF16), wo_ref[...])
    ms = jnp.mean(x1 * x1, axis=-1, keepdims=True)
    xn2 = (x1 * lax.rsqrt(ms + EPS) * n2_ref[...]).astype(BF16)
    acc_ref[...] = x1

    if not sample:
        @pl.when(pl.program_id(0) % tiles_per_seq == 0)
        def _():
            gcar_ref[...] = jnp.zeros_like(gcar_ref)

    for c in range(N_FF_CHUNKS):
        cols = slice(c * FF_CHUNK, (c + 1) * FF_CHUNK)
        g = _dot(xn2, wup_ref[:, cols])
        u = _dot(xn2, wup_ref[:, D_FF + c * FF_CHUNK:D_FF + (c + 1) * FF_CHUNK])
        w0, w1, w2 = fcw_ref[0:1, cols], fcw_ref[1:2, cols], fcw_ref[2:3, cols]
        if sample:
            gc = fs0_ref[:, cols] * w0 + fs1_ref[:, cols] * w1 + g * w2
            g_ref[:, cols] = g
        else:
            ge = jnp.concatenate([gcar_ref[:, cols], g], axis=0)
            gc = ge[6:6 + tm] * w0 + ge[7:7 + tm] * w1 + g * w2
            gcar_ref[:, cols] = g[tm - 8:tm]
            fst_ref[0, :, cols] = g[tm - 2:tm]
        hidden = (gc * _sigmoid(gc) * u).astype(BF16)
        acc_ref[...] += _dot(hidden, wdn_ref[cols, :])
    y_ref[...] = acc_ref[...]

    if not sample:
        pltpu.make_async_copy(ck_ref.at[0, pl.ds(0, pages_per_step)], pbuf, psem.at[0]).wait()
        q2s = _block_diag_query(qs_ref[0])
        for i in range(pages_per_step):
            sc_ref[i] = _page_scores(q2s, pbuf[i])


def _merge_ffn(x, attn, s_in, ga, gb, wa, wc, wo, n2, wup, fcw, wdn, states, *, sample, seq_len=1, stream=None):
    n_tok = x.shape[0]
    tm = n_tok if sample else TOKEN_TILE
    n_tiles = n_tok // tm
    tiles_per_seq = 1 if sample else seq_len // tm

    def row(width):
        return pl.BlockSpec((tm, width), lambda i, *_: (i, 0))

    in_specs = [row(D_MODEL), row(ATTN_W), row(CONV_CH), row(D_MODEL), row(D_MODEL),
                _const_spec((ATTN_W, D_MODEL)), _const_spec((CONV_CH, D_MODEL)),
                _const_spec((D_MODEL, D_MODEL)), _const_spec((1, D_MODEL)),
                _const_spec((D_MODEL, 2 * D_FF)), _const_spec((3, D_FF)), _const_spec((D_FF, D_MODEL))]
    args = [x, attn, s_in, ga, gb, wa, wc, wo, n2, wup, fcw, wdn]
    scratch = [pltpu.VMEM((tm, D_MODEL), F32)]
    if sample:
        in_specs += [row(D_FF), row(D_FF)]
        args += list(states)
        out_specs = [row(D_MODEL), row(D_FF)]
        out_shape = [jax.ShapeDtypeStruct((n_tok, D_MODEL), F32),
                     jax.ShapeDtypeStruct((n_tok, D_FF), F32)]
    else:
        n_seq = n_tiles // tiles_per_seq
        page_table_flat, q_sample, cache_kt, n_pages, first_page = stream
        pps = FFN_STREAM_PAGES
        assert first_page % n_pages == 0 and n_pages % pps == 0
        in_specs += [pl.BlockSpec((1, 1, ATTN_W), lambda i, *_: ((first_page + i * pps) // n_pages, 0, 0)),
                     pl.BlockSpec(memory_space=pl.ANY)]
        args = [page_table_flat] + args + [q_sample, cache_kt]
        out_specs = [row(D_MODEL), pl.BlockSpec((1, 2, D_FF), lambda i, *_: (i // tiles_per_seq, 0, 0)),
                     pl.BlockSpec((pps, N_HEADS, PAGE), lambda i, *_: (i, 0, 0))]
        out_shape = [jax.ShapeDtypeStruct((n_tok, D_MODEL), F32),
                     jax.ShapeDtypeStruct((n_seq, 2, D_FF), F32),
                     jax.ShapeDtypeStruct((n_tiles * pps, N_HEADS, PAGE), F32)]
        scratch += [pltpu.VMEM((8, D_FF), F32),
                    pltpu.VMEM((pps, N_HEADS, HEAD_DIM, PAGE), F32),
                    pltpu.SemaphoreType.DMA((1,))]
    return pl.pallas_call(
        functools.partial(_merge_ffn_kernel, tm=tm, sample=sample, tiles_per_seq=tiles_per_seq,
                          stream=None if sample else (stream[4], FFN_STREAM_PAGES)),
        grid_spec=pltpu.PrefetchScalarGridSpec(
            num_scalar_prefetch=0 if sample else 1,
            grid=(n_tiles,),
            in_specs=in_specs,
            out_specs=out_specs,
            scratch_shapes=scratch,
        ),
        out_shape=out_shape,
        compiler_params=pltpu.CompilerParams(dimension_semantics=("arbitrary",),
                                             vmem_limit_bytes=VMEM_LIMIT),
    )(*args)


SEQ_PER_SELECT_STEP = 8


def _select_kernel(sc_ref, idx_ref):
    n_pages = sc_ref.shape[1]
    pages_per_block = BLOCK // PAGE
    n_blocks = n_pages // pages_per_block
    blk = lax.broadcasted_iota(jnp.int32, (n_blocks, N_HEADS, 128), 0)
    for s in range(SEQ_PER_SELECT_STEP):
        page_sum = jnp.sum(sc_ref[s], axis=-1, keepdims=True)
        block_sum = jnp.sum(page_sum.reshape(n_blocks, pages_per_block, N_HEADS, 1), axis=1)
        gate = jnp.broadcast_to(block_sum * (1.0 / BLOCK), (n_blocks, N_HEADS, 128))
        for r in range(TOP_K):
            best = jnp.max(gate, axis=0, keepdims=True)
            idx = jnp.min(jnp.where(gate == best, blk, n_blocks), axis=0, keepdims=True)
            idx_ref[s, r] = idx[0]
            gate = jnp.where(blk == idx, -jnp.inf, gate)


def _select_blocks(scores):
    n_seq, n_pages = scores.shape[:2]
    assert n_seq % SEQ_PER_SELECT_STEP == 0
    return pl.pallas_call(
        _select_kernel,
        grid=(n_seq // SEQ_PER_SELECT_STEP,),
        in_specs=[pl.BlockSpec((SEQ_PER_SELECT_STEP, n_pages, N_HEADS, PAGE), lambda i: (i, 0, 0, 0))],
        out_specs=pl.BlockSpec((SEQ_PER_SELECT_STEP, TOP_K, N_HEADS, 128), lambda i: (i, 0, 0, 0)),
        out_shape=jax.ShapeDtypeStruct((n_seq, TOP_K, N_HEADS, 128), jnp.int32),
    )(scores)


PAGES_PER_BLOCK = BLOCK // PAGE
CHUNKS_PER_HEAD = TOP_K * PAGES_PER_BLOCK
N_CHUNKS = N_HEADS * CHUNKS_PER_HEAD
KEYS_PER_HEAD = CHUNKS_PER_HEAD * PAGE


def _sample_attn_kernel(pt_ref, idx_ref, sc0_ref, sc1_ref, sc2_ref, q_ref, kn_ref, vn_ref, brow_ref, ownb_ref,
                        cv_ref, o_ref, vbuf, logit_ref, sem, *, n_seq, n_pages, seq_splits):
    b = pl.program_id(0)
    slot = b % 2

    def score_tile(logical):
        in_later = jnp.where(b < seq_splits[1], sc1_ref[0, logical], sc2_ref[0, logical])
        return jnp.where(b < seq_splits[0], sc0_ref[0, logical], in_later)

    def picked_page(seq, h, r, j):
        n = idx_ref[(seq * TOP_K + r) * N_HEADS + h]
        return n, n * PAGES_PER_BLOCK + j

    def copies(seq, slot_):
        out = []
        for h in range(N_HEADS):
            for r in range(TOP_K):
                for j in range(PAGES_PER_BLOCK):
                    _, logical = picked_page(seq, h, r, j)
                    page = pt_ref[seq * n_pages + logical]
                    c = (h * TOP_K + r) * PAGES_PER_BLOCK + j
                    out.append(pltpu.make_async_copy(cv_ref.at[0, page, h], vbuf.at[slot_, c], sem.at[slot_]))
        return out

    @pl.when(b == 0)
    def _():
        for c in copies(0, 0):
            c.start()

    @pl.when(b + 1 < n_seq)
    def _():
        for c in copies(b + 1, 1 - slot):
            c.start()

    last_block = n_pages // PAGES_PER_BLOCK - 1
    for h in range(N_HEADS):
        for r in range(TOP_K):
            for j in range(PAGES_PER_BLOCK):
                n, logical = picked_page(b, h, r, j)
                piece = score_tile(logical)[h:h + 1, :]
                near = brow_ref[h][:, j * PAGE:(j + 1) * PAGE]
                piece = piece + jnp.where(n == last_block, near, jnp.zeros_like(near))
                c = r * PAGES_PER_BLOCK + j
                logit_ref[h:h + 1, c * PAGE:(c + 1) * PAGE] = piece
    logits = logit_ref[...]

    q = q_ref[0]
    own = jnp.sum(q * kn_ref[0], axis=-1, keepdims=True) + ownb_ref[:, 0, 0:1]
    m = jnp.maximum(jnp.max(logits, axis=-1, keepdims=True), own)
    p = jnp.exp(logits - m)
    p_own = jnp.exp(own - m)
    denom = jnp.sum(p, axis=-1, keepdims=True) + p_own

    p_wide = jnp.concatenate([p] * N_HEADS, axis=1)
    col = lax.broadcasted_iota(jnp.int32, p_wide.shape, 1)
    row = lax.broadcasted_iota(jnp.int32, p_wide.shape, 0)
    p_wide = jnp.where(col // KEYS_PER_HEAD == row, p_wide, 0.0)
    p16 = jnp.concatenate([p_wide, jnp.zeros_like(p_wide)], axis=0).astype(BF16)

    pltpu.make_async_copy(cv_ref.at[0, pl.ds(0, N_CHUNKS), 0], vbuf.at[slot], sem.at[slot]).wait()
    vt = jnp.concatenate([vbuf[slot, c] for c in range(N_CHUNKS)], axis=1).astype(BF16)
    ctx = _dot_nt(p16, vt)[:N_HEADS] + p_own * vn_ref[0]
    o_ref[0] = ctx / denom


def _sample_attention(page_table_flat, idx_flat, score_parts, q_heads, k_new, v_new, brow, ownb, cache_vt, n_pages):
    n_seq = q_heads.shape[0]
    head_spec = pl.BlockSpec((1, N_HEADS, HEAD_DIM), lambda b, pt, ix: (b, 0, 0))
    starts = [0, score_parts[0].shape[0], score_parts[0].shape[0] + score_parts[1].shape[0]]

    def part_spec(k):
        last = score_parts[k].shape[0] - 1
        return pl.BlockSpec((1, n_pages, N_HEADS, PAGE),
                            lambda b, pt, ix: (jnp.clip(b - starts[k], 0, last), 0, 0, 0))

    return pl.pallas_call(
        functools.partial(_sample_attn_kernel, n_seq=n_seq, n_pages=n_pages, seq_splits=(starts[1], starts[2])),
        grid_spec=pltpu.PrefetchScalarGridSpec(
            num_scalar_prefetch=2,
            grid=(n_seq,),
            in_specs=[part_spec(0), part_spec(1), part_spec(2),
                      head_spec, head_spec, head_spec,
                      pl.BlockSpec((N_HEADS, 1, BLOCK), lambda b, pt, ix: (0, 0, 0)),
                      pl.BlockSpec((N_HEADS, 1, 128), lambda b, pt, ix: (0, 0, 0)),
                      pl.BlockSpec(memory_space=pl.ANY)],
            out_specs=head_spec,
            scratch_shapes=[pltpu.VMEM((2, N_CHUNKS, HEAD_DIM, PAGE), F32),
                            pltpu.VMEM((N_HEADS, KEYS_PER_HEAD), F32),
                            pltpu.SemaphoreType.DMA((2,))],
        ),
        out_shape=jax.ShapeDtypeStruct((n_seq, N_HEADS, HEAD_DIM), F32),
        compiler_params=pltpu.CompilerParams(dimension_semantics=("arbitrary",),
                                             vmem_limit_bytes=VMEM_LIMIT),
    )(page_table_flat, idx_flat, *score_parts, q_heads, k_new, v_new, brow, ownb, cache_vt)


def kernel(x_prompt, x_sample, cache_k, cache_v, page_table, state_conv, state_ffn, norm1_w, w_in, q_norm_w, k_norm_w, conv_w, w_attn_up, w_conv_out, w_o, norm2_w, w_ffn_up, ffn_conv_w, w_ffn_down, rel_bias):
    batch, seq, _ = x_prompt.shape
    n_seq = x_sample.shape[0]
    n_pages = page_table.shape[1]

    w_in_b = w_in[0].astype(BF16)
    wa = w_attn_up[0].astype(BF16)
    wc = w_conv_out[0].astype(BF16)
    wo = w_o[0].astype(BF16)
    wup = w_ffn_up[0].astype(BF16)
    wdn = w_ffn_down[0].astype(BF16)
    fcw = ffn_conv_w[0]
    n1 = norm1_w[0].reshape(1, D_MODEL)
    n2 = norm2_w[0].reshape(1, D_MODEL)
    qn = jnp.tile(q_norm_w[0], N_HEADS).reshape(1, ATTN_W)
    kn = jnp.tile(k_norm_w[0], N_HEADS).reshape(1, ATTN_W)
    lane = np.arange(BLOCK)
    bd = jnp.asarray((lane[:, None] // HEAD_DIM == lane[None, :] // HEAD_DIM) / HEAD_DIM, BF16)
    cw = conv_w[0]

    bias_tiles, bias_row, bias_own = _bias_tables(rel_bias)

    xs = x_sample.reshape(n_seq, D_MODEL)
    (q_s, k_s, v_s, sin_s, ga_s, gb_s, u_s) = _inproj(
        xs, n1, w_in_b, qn, kn, bd, cw, (state_conv[0, :, 0, :], state_conv[0, :, 1, :]), sample=True)
    pt_flat = page_table.reshape(-1)
    cache_kt = cache_k.transpose(0, 1, 3, 4, 2)
    cache_vt = cache_v.transpose(0, 1, 3, 4, 2)
    xp = x_prompt.reshape(batch * seq, D_MODEL)
    n_tiles = batch * seq // TOKEN_TILE
    n_ffn_pages = n_tiles * FFN_STREAM_PAGES
    n_attn_pages = n_seq * n_pages - n_tiles * INPROJ_STREAM_PAGES - n_ffn_pages
    (q_b, kt_p, vt_p, k_b, vt_b, ksum_p, sin_p, ga_p, gb_p, conv_p, scores_2) = _inproj(
        xp, n1, w_in_b, qn, kn, bd, cw, None, sample=False, seq_len=seq,
        stream=(pt_flat, q_s.reshape(n_seq, 1, ATTN_W), cache_kt, n_pages, n_attn_pages + n_ffn_pages))

    attn_p, scores_0 = _prompt_attention(q_b.reshape(batch, seq, ATTN_W), k_b.reshape(batch, seq, ATTN_W), vt_b,
                                         ksum_p.reshape(batch, seq // BLOCK, ATTN_W), bias_tiles,
                                         pt_flat, q_s, cache_kt, n_pages, n_attn_pages)
    y_p, ffn_p, scores_1 = _merge_ffn(xp, attn_p.reshape(batch * seq, ATTN_W), sin_p, ga_p, gb_p,
                                      wa, wc, wo, n2, wup, fcw, wdn, None, sample=False, seq_len=seq,
                                      stream=(pt_flat, q_s.reshape(n_seq, 1, ATTN_W), cache_kt, n_pages, n_attn_pages))

    score_parts = [sc.reshape(-1, n_pages, N_HEADS, PAGE) for sc in (scores_0, scores_1, scores_2)]
    idx = jnp.concatenate([_select_blocks(sc)[..., 0] for sc in score_parts], axis=0)
    attn_s = _sample_attention(pt_flat, idx.reshape(-1), score_parts, q_s.reshape(n_seq, N_HEADS, HEAD_DIM),
                               k_s.reshape(n_seq, N_HEADS, HEAD_DIM), v_s.reshape(n_seq, N_HEADS, HEAD_DIM),
                               bias_row, bias_own, cache_vt, n_pages)

    y_s, g_s = _merge_ffn(xs, attn_s.reshape(n_seq, ATTN_W), sin_s, ga_s, gb_s,
                          wa, wc, wo, n2, wup, fcw, wdn,
                          (state_ffn[0, :, 0, :], state_ffn[0, :, 1, :]), sample=True)

    return (
        y_p.reshape(batch, seq, D_MODEL),
        y_s.reshape(n_seq, 1, D_MODEL),
        kt_p.reshape(1, batch, N_HEADS, HEAD_DIM, seq).transpose(0, 1, 4, 2, 3),
        vt_p.reshape(1, batch, N_HEADS, HEAD_DIM, seq).transpose(0, 1, 4, 2, 3),
        conv_p[None],
        ffn_p[None],
        k_s.reshape(1, n_seq, 1, N_HEADS, HEAD_DIM),
        v_s.reshape(1, n_seq, 1, N_HEADS, HEAD_DIM),
        jnp.stack([state_conv[0, :, 1, :], u_s], axis=1)[None],
        jnp.stack([state_ffn[0, :, 1, :], g_s], axis=1)[None],
    )
```

```python
import functools
import math

import numpy as np
import jax
import jax.numpy as jnp
from jax import lax
from jax.experimental import pallas as pl
from jax.experimental.pallas import tpu as pltpu

F32 = jnp.float32
BF16 = jnp.bfloat16

D_MODEL = 1024
N_HEADS = 8
HEAD_DIM = 64
ATTN_W = N_HEADS * HEAD_DIM
CONV_CH = D_MODEL // 2
D_FF = 2816
BLOCK = 256
TOP_K = 3
PAGE = 128
NUM_BUCKETS = 32
MAX_DISTANCE = 128
EPS = 1e-6
N_COLS = 3 * ATTN_W + 3 * CONV_CH + 2 * D_MODEL

FF_CHUNK = 256
N_FF_CHUNKS = D_FF // FF_CHUNK
TOKEN_TILE = 512
NEG_BIG = -1e30
VMEM_LIMIT = 60 * 1024 * 1024
LOG2E = math.log2(math.e)


def _bucket_thresholds():
    max_exact = NUM_BUCKETS // 2
    d = np.arange(0, 4 * MAX_DISTANCE)
    logd = np.log(np.maximum(d, 1) / max_exact)
    large = max_exact + (logd / math.log(MAX_DISTANCE / max_exact) * (NUM_BUCKETS - max_exact)).astype(np.int32)
    bucket = np.where(d < max_exact, d, np.minimum(large, NUM_BUCKETS - 1))
    return [int(np.argmax(bucket >= b)) for b in range(NUM_BUCKETS)]


BUCKET_THR = _bucket_thresholds()


def _dot(a, b):
    return jnp.dot(a, b, preferred_element_type=F32)


def _dot_nt(a, b):
    return lax.dot_general(a, b, (((1,), (1,)), ((), ())), preferred_element_type=F32)


def _sigmoid(x):
    return 0.5 * jnp.tanh(0.5 * x) + 0.5


def _const_spec(shape):
    nd = len(shape)
    return pl.BlockSpec(shape, lambda *_: (0,) * nd, pipeline_mode=pl.Buffered(1))


def _bias_from_distance(d, rb_ref, h, b_far):
    val = jnp.zeros(d.shape, F32)
    for b in range(NUM_BUCKETS - 2, -1, -1):
        val = jnp.where(d < BUCKET_THR[b + 1], rb_ref[b, h] - b_far, val)
    return val


def _bias_kernel(rb_ref, tile_ref, row_ref, own_ref):
    hp = pl.program_id(0)
    j = lax.broadcasted_iota(jnp.int32, (BLOCK, BLOCK), 0)
    i = lax.broadcasted_iota(jnp.int32, (BLOCK, BLOCK), 1)
    jr = lax.broadcasted_iota(jnp.int32, (1, BLOCK), 1)
    for e in range(2):
        h = 2 * hp + e
        b_far = rb_ref[NUM_BUCKETS - 1, h]
        lanes = slice(e * BLOCK, (e + 1) * BLOCK)
        tile_ref[0, 0, :, lanes] = jnp.where(j <= i, _bias_from_distance(i - j, rb_ref, h, b_far) * LOG2E, NEG_BIG)
        tile_ref[0, 1, :, lanes] = _bias_from_distance(i - j + BLOCK, rb_ref, h, b_far) * LOG2E
        row_ref[e] = _bias_from_distance(BLOCK - jr, rb_ref, h, b_far)
        own_ref[e] = jnp.zeros((1, 128), F32) + (rb_ref[0, h] - b_far)
    tile_ref[0, 2] = jnp.zeros((BLOCK, 2 * BLOCK), F32)


def _bias_tables(rel_bias):
    n_hp = N_HEADS // 2
    return pl.pallas_call(
        _bias_kernel,
        grid=(n_hp,),
        in_specs=[pl.BlockSpec(memory_space=pltpu.SMEM)],
        out_specs=[
            pl.BlockSpec((1, 3, BLOCK, 2 * BLOCK), lambda hp: (hp, 0, 0, 0)),
            pl.BlockSpec((2, 1, BLOCK), lambda hp: (hp, 0, 0)),
            pl.BlockSpec((2, 1, 128), lambda hp: (hp, 0, 0)),
        ],
        out_shape=[
            jax.ShapeDtypeStruct((n_hp, 3, BLOCK, 2 * BLOCK), F32),
            jax.ShapeDtypeStruct((N_HEADS, 1, BLOCK), F32),
            jax.ShapeDtypeStruct((N_HEADS, 1, 128), F32),
        ],
    )(rel_bias)


def _inproj_kernel(*refs, tm, sample, tiles_per_seq):
    if sample:
        (x_ref, n1_ref, w_ref, qn_ref, kn_ref, bd_ref, cw_ref, st0_ref, st1_ref,
         q_ref, k_ref, v_ref, sin_ref, ga_ref, gb_ref, u_ref) = refs
    else:
        (x_ref, n1_ref, w_ref, qn_ref, kn_ref, bd_ref, cw_ref,
         q_ref, kt_ref, vt_ref, kb_ref, vtb_ref, ksum_ref, sin_ref, ga_ref, gb_ref, cst_ref,
         ucar_ref) = refs

    x = x_ref[...]
    ms = jnp.mean(x * x, axis=-1, keepdims=True)
    xn = (x * lax.rsqrt(ms + EPS) * n1_ref[...]).astype(BF16)

    def proj(a, b):
        return _dot(xn, w_ref[:, a:b])

    def head_norm(t, w_row):
        sq = (t * t).astype(BF16)
        bd = bd_ref[...]
        half = ATTN_W // 2
        msq = jnp.concatenate([_dot(sq[:, :half], bd), _dot(sq[:, half:], bd)], axis=1)
        return t * lax.rsqrt(msq + EPS) * w_row

    q = head_norm(proj(0, ATTN_W), qn_ref[...]) * (HEAD_DIM ** -0.5)
    k = head_norm(proj(ATTN_W, 2 * ATTN_W), kn_ref[...])
    v = proj(2 * ATTN_W, 3 * ATTN_W)
    c0 = 3 * ATTN_W
    cb = proj(c0, c0 + CONV_CH)
    u = proj(c0 + CONV_CH, c0 + 2 * CONV_CH) * proj(c0 + 2 * CONV_CH, c0 + 3 * CONV_CH)
    g0 = c0 + 3 * CONV_CH
    ga_ref[...] = proj(g0, g0 + D_MODEL).astype(ga_ref.dtype)
    gb_ref[...] = proj(g0 + D_MODEL, g0 + 2 * D_MODEL).astype(gb_ref.dtype)

    w0, w1, w2 = cw_ref[0:1, :], cw_ref[1:2, :], cw_ref[2:3, :]
    if sample:
        q_ref[...] = q
        k_ref[...] = k
        v_ref[...] = v
        uc = st0_ref[...] * w0 + st1_ref[...] * w1 + u * w2
        u_ref[...] = u
    else:
        q_ref[...] = (q * LOG2E).astype(BF16)
        kb_ref[...] = k.astype(BF16)
        kt_ref[0] = k.T
        v_t = v.T
        vt_ref[0] = v_t
        for r in range(tm // BLOCK):
            vtb_ref[0, r] = v_t[:, r * BLOCK:(r + 1) * BLOCK].astype(BF16)
            ksum_ref[0, r:r + 1, :] = jnp.sum(k[r * BLOCK:(r + 1) * BLOCK], axis=0, keepdims=True)

        @pl.when(pl.program_id(0) % tiles_per_seq == 0)
        def _():
            ucar_ref[...] = jnp.zeros_like(ucar_ref)

        ue = jnp.concatenate([ucar_ref[...], u], axis=0)
        uc = ue[6:6 + tm] * w0 + ue[7:7 + tm] * w1 + u * w2
        ucar_ref[...] = u[tm - 8:tm]
        cst_ref[0] = u[tm - 2:tm]
    sin_ref[...] = (cb * uc).astype(sin_ref.dtype)


def _inproj(x, n1, w_in, qn, kn, bd, conv_w, states, *, sample, seq_len=1):
    n_tok = x.shape[0]
    tm = n_tok if sample else TOKEN_TILE
    n_tiles = n_tok // tm
    tiles_per_seq = 1 if sample else seq_len // tm

    def row(width):
        return pl.BlockSpec((tm, width), lambda i, *_: (i, 0))

    in_specs = [row(D_MODEL), _const_spec((1, D_MODEL)), _const_spec((D_MODEL, N_COLS)),
                _const_spec((1, ATTN_W)), _const_spec((1, ATTN_W)), _const_spec((BLOCK, BLOCK)),
                _const_spec((3, CONV_CH))]
    args = [x, n1, w_in, qn, kn, bd, conv_w]
    if sample:
        in_specs += [row(CONV_CH), row(CONV_CH)]
        args += list(states)
        out_specs = [row(ATTN_W), row(ATTN_W), row(ATTN_W), row(CONV_CH), row(D_MODEL), row(D_MODEL),
                     row(CONV_CH)]
        out_shape = [jax.ShapeDtypeStruct((n_tok, ATTN_W), F32)] * 3 + [
            jax.ShapeDtypeStruct((n_tok, CONV_CH), BF16),
            jax.ShapeDtypeStruct((n_tok, D_MODEL), BF16),
            jax.ShapeDtypeStruct((n_tok, D_MODEL), BF16),
            jax.ShapeDtypeStruct((n_tok, CONV_CH), F32)]
        scratch = []
    else:
        blocks_per_tile = tm // BLOCK
        n_seq = n_tiles // tiles_per_seq
        t_spec = pl.BlockSpec((1, ATTN_W, tm), lambda i, *_: (i // tiles_per_seq, 0, i % tiles_per_seq))
        out_specs = [row(ATTN_W), t_spec, t_spec, row(ATTN_W),
                     pl.BlockSpec((1, blocks_per_tile, ATTN_W, BLOCK),
                                  lambda i, *_: (i // tiles_per_seq, i % tiles_per_seq, 0, 0)),
                     pl.BlockSpec((1, blocks_per_tile, ATTN_W), lambda i, *_: (i, 0, 0)),
                     row(CONV_CH), row(D_MODEL), row(D_MODEL),
                     pl.BlockSpec((1, 2, CONV_CH), lambda i, *_: (i // tiles_per_seq, 0, 0))]
        out_shape = [jax.ShapeDtypeStruct((n_tok, ATTN_W), BF16),
                     jax.ShapeDtypeStruct((n_seq, ATTN_W, seq_len), F32),
                     jax.ShapeDtypeStruct((n_seq, ATTN_W, seq_len), F32),
                     jax.ShapeDtypeStruct((n_tok, ATTN_W), BF16),
                     jax.ShapeDtypeStruct((n_seq, seq_len // BLOCK, ATTN_W, BLOCK), BF16),
                     jax.ShapeDtypeStruct((n_tiles, blocks_per_tile, ATTN_W), F32),
                     jax.ShapeDtypeStruct((n_tok, CONV_CH), BF16),
                     jax.ShapeDtypeStruct((n_tok, D_MODEL), BF16),
                     jax.ShapeDtypeStruct((n_tok, D_MODEL), BF16),
                     jax.ShapeDtypeStruct((n_tiles // tiles_per_seq, 2, CONV_CH), F32)]
        scratch = [pltpu.VMEM((8, CONV_CH), F32)]
    return pl.pallas_call(
        functools.partial(_inproj_kernel, tm=tm, sample=sample, tiles_per_seq=tiles_per_seq),
        grid=(n_tiles,),
        in_specs=in_specs,
        out_specs=out_specs,
        scratch_shapes=scratch,
        out_shape=out_shape,
        compiler_params=pltpu.CompilerParams(dimension_semantics=("arbitrary",),
                                             vmem_limit_bytes=VMEM_LIMIT),
    )(*args)


STREAM_SLOTS = 3
FFN_STREAM_PAGES = 32


def _block_diag_query(q_row):
    row = lax.broadcasted_iota(jnp.int32, (N_HEADS, ATTN_W), 0)
    lane_head = lax.broadcasted_iota(jnp.int32, (N_HEADS, ATTN_W), 1) // HEAD_DIM
    q_rows = jnp.where(row == lane_head, q_row, 0.0)
    q_hi = q_rows.astype(BF16).astype(F32)
    return jnp.concatenate([q_hi, q_rows - q_hi], axis=0).astype(BF16)


def _page_scores(q2, page):
    s2 = _dot(q2, page.reshape(ATTN_W, PAGE).astype(BF16))
    return s2[:N_HEADS] + s2[N_HEADS:]


def _attn_kernel(pt_ref, q_ref, k_ref, vt_ref, ksum_ref, bias_ref, qs_ref, ck_ref, o_ref, sc_ref,
                 pen_ref, m_ref, l_ref, alpha_ref, acc_ref, s_ref, p_ref, pbuf, psem, *, pages_per_step, n_pages):
    qb = pl.program_id(2)
    n_blocks = k_ref.shape[1] // BLOCK
    nq = 2 * BLOCK
    step = (pl.program_id(0) * pl.num_programs(1) + pl.program_id(1)) * pl.num_programs(2) + qb
    n_steps = pl.num_programs(0) * pl.num_programs(1) * pl.num_programs(2)

    def page_copies(step_, slot_):
        return [pltpu.make_async_copy(ck_ref.at[0, pt_ref[step_ * pages_per_step + i]], pbuf.at[slot_, i],
                                      psem.at[slot_]) for i in range(pages_per_step)]

    @pl.when(step == 0)
    def _():
        for ahead in range(STREAM_SLOTS - 1):
            for c in page_copies(ahead, ahead):
                c.start()

    @pl.when(step + (STREAM_SLOTS - 1) < n_steps)
    def _():
        nxt = step + (STREAM_SLOTS - 1)
        for c in page_copies(nxt, nxt % STREAM_SLOTS):
            c.start()

    slot = step % STREAM_SLOTS
    pltpu.make_async_copy(ck_ref.at[0, pl.ds(0, pages_per_step)], pbuf.at[slot], psem.at[slot]).wait()
    page0 = step * pages_per_step
    seq0 = page0 // n_pages
    seq1 = jnp.minimum(seq0 + 1, qs_ref.shape[0] - 1)
    q2s0 = _block_diag_query(qs_ref[pl.ds(seq0, 1), :])
    q2s1 = _block_diag_query(qs_ref[pl.ds(seq1, 1), :])

    def score_page(i):
        q2s = jnp.where((page0 + i) // n_pages == seq0, q2s0, q2s1)
        sc_ref[i] = _page_scores(q2s, pbuf[slot, i])

    pages_at_head = (pages_per_step * 5) // 8
    for i in range(pages_at_head):
        score_page(i)

    q = q_ref[0]
    lane_head = lax.broadcasted_iota(jnp.int32, q.shape, 1) // HEAD_DIM
    zero = jnp.zeros_like(q)
    q2 = jnp.concatenate([jnp.where(lane_head == 0, q, zero), jnp.where(lane_head == 1, q, zero)], axis=0)

    kmean = ksum_ref[0] * (1.0 / BLOCK)
    km_hi = kmean.astype(BF16)
    km_lo = (kmean - km_hi.astype(F32)).astype(BF16)
    g2 = _dot_nt(jnp.concatenate([km_hi, km_lo], axis=0), q2)
    g = g2[:n_blocks] + g2[n_blocks:]
    blk = lax.broadcasted_iota(jnp.int32, (n_blocks, nq), 0)
    valid = blk < qb
    g = jnp.where(valid, g, -jnp.inf)
    rank = jnp.zeros(g.shape, jnp.int32)
    for j in range(n_blocks):
        gj = g[j:j + 1, :]
        beats = (gj > g) | ((gj == g) & (blk > j))
        rank = rank + beats.astype(jnp.int32)
    sel = valid & (rank < TOP_K)
    pen_ref[0:n_blocks] = jnp.where(sel, 0.0, NEG_BIG)
    pen_ref[n_blocks:2 * n_blocks] = jnp.zeros((n_blocks, nq), F32)

    def item(j):
        own = j == 0
        kb = jnp.where(own, qb, j - 1)
        tile = jnp.where(own, 0, jnp.where(j == qb, 1, 2))
        pen_row = jnp.where(own, n_blocks, j - 1)
        return kb, tile, pen_row

    def item_scores(j):
        kb, tile, pen_row = item(j)
        kblk = k_ref[0, pl.ds(pl.multiple_of(kb * BLOCK, BLOCK), BLOCK), :]
        return _dot_nt(kblk, q2) + pen_ref[pl.ds(pen_row, 1), :] + bias_ref[0, tile]

    def pv(kb, p):
        vt = vt_ref[0, kb]
        return jnp.concatenate([_dot(vt[:HEAD_DIM], p[:, :BLOCK]), _dot(vt[HEAD_DIM:], p[:, BLOCK:])], axis=1)

    n_items = qb + 1
    s_ref[...] = item_scores(0)
    m_ref[...] = jnp.full(m_ref.shape, NEG_BIG, F32)
    l_ref[...] = jnp.zeros(l_ref.shape, F32)
    alpha_ref[...] = jnp.ones(alpha_ref.shape, F32)
    acc_ref[...] = jnp.zeros(acc_ref.shape, F32)
    p_ref[...] = jnp.zeros(p_ref.shape, BF16)

    def trip(j, carry):
        s = s_ref[...]
        s_next = item_scores(jnp.minimum(j + 1, n_items - 1))
        kb_prev, _, _ = item(jnp.maximum(j - 1, 0))
        acc = alpha_ref[...] * acc_ref[...] + pv(kb_prev, p_ref[...])
        m_old = m_ref[...]
        m_new = jnp.maximum(m_old, jnp.max(s, axis=0, keepdims=True))
        alpha = jnp.exp2(m_old - m_new)
        p = jnp.exp2(s - m_new)
        l_ref[...] = alpha * l_ref[...] + jnp.sum(p, axis=0, keepdims=True)
        m_ref[...] = m_new
        acc_ref[...] = acc
        alpha_ref[...] = alpha
        p_ref[...] = p.astype(BF16)
        s_ref[...] = s_next
        return carry

    lax.fori_loop(0, n_items, trip, 0)
    kb_last, _, _ = item(qb)
    acc = alpha_ref[...] * acc_ref[...] + pv(kb_last, p_ref[...])
    out = acc / l_ref[...]
    out_t = jnp.concatenate([out[:, :BLOCK], out[:, BLOCK:]], axis=0)
    o_ref[0] = out_t.T.astype(o_ref.dtype)

    for i in range(pages_at_head, pages_per_step):
        score_page(i)


def _prompt_attention(q, k, vt_blocks, ksum, bias_tiles, page_table_flat, q_sample, cache_kt, n_pages,
                      n_stream_pages):
    b, s, _ = q.shape
    n_blocks = s // BLOCK
    n_hp = N_HEADS // 2
    n_steps = b * n_hp * n_blocks
    n_seq = q_sample.shape[0]
    pages_per_step = n_stream_pages // n_steps
    assert pages_per_step * n_steps == n_stream_pages and pages_per_step <= n_pages

    def step_of(bi, hp, qb):
        return (bi * n_hp + hp) * n_blocks + qb

    attn, scores = pl.pallas_call(
        functools.partial(_attn_kernel, pages_per_step=pages_per_step, n_pages=n_pages),
        grid_spec=pltpu.PrefetchScalarGridSpec(
            num_scalar_prefetch=1,
            grid=(b, n_hp, n_blocks),
            in_specs=[
                pl.BlockSpec((1, BLOCK, 128), lambda bi, hp, qb, pt: (bi, qb, hp)),
                pl.BlockSpec((1, s, 128), lambda bi, hp, qb, pt: (bi, 0, hp)),
                pl.BlockSpec((1, n_blocks, 128, BLOCK), lambda bi, hp, qb, pt: (bi, 0, hp, 0)),
                pl.BlockSpec((1, n_blocks, 128), lambda bi, hp, qb, pt: (bi, 0, hp)),
                pl.BlockSpec((1, 3, BLOCK, 2 * BLOCK), lambda bi, hp, qb, pt: (hp, 0, 0, 0)),
                pl.BlockSpec((n_seq, ATTN_W), lambda bi, hp, qb, pt: (0, 0), pipeline_mode=pl.Buffered(1)),
                pl.BlockSpec(memory_space=pl.ANY),
            ],
            out_specs=[
                pl.BlockSpec((1, BLOCK, 128), lambda bi, hp, qb, pt: (bi, qb, hp)),
                pl.BlockSpec((pages_per_step, N_HEADS, PAGE), lambda bi, hp, qb, pt: (step_of(bi, hp, qb), 0, 0)),
            ],
            scratch_shapes=[
                pltpu.VMEM((2 * n_blocks, 2 * BLOCK), F32),
                pltpu.VMEM((1, 2 * BLOCK), F32),
                pltpu.VMEM((1, 2 * BLOCK), F32),
                pltpu.VMEM((1, 2 * BLOCK), F32),
                pltpu.VMEM((HEAD_DIM, 2 * BLOCK), F32),
                pltpu.VMEM((BLOCK, 2 * BLOCK), F32),
                pltpu.VMEM((BLOCK, 2 * BLOCK), BF16),
                pltpu.VMEM((STREAM_SLOTS, pages_per_step, N_HEADS, HEAD_DIM, PAGE), F32),
                pltpu.SemaphoreType.DMA((STREAM_SLOTS,)),
            ],
        ),
        out_shape=[jax.ShapeDtypeStruct((b, s, ATTN_W), BF16),
                   jax.ShapeDtypeStruct((n_stream_pages, N_HEADS, PAGE), F32)],
        compiler_params=pltpu.CompilerParams(
            dimension_semantics=("arbitrary", "arbitrary", "arbitrary"),
            vmem_limit_bytes=VMEM_LIMIT),
    )(page_table_flat, q, k, vt_blocks, ksum, bias_tiles, q_sample, cache_kt)
    return attn, scores


def _merge_ffn_kernel(*refs, tm, sample, tiles_per_seq, stream):
    if sample:
        (x_ref, at_ref, sin_ref, ga_ref, gb_ref, wa_ref, wc_ref, wo_ref, n2_ref, wup_ref, fcw_ref,
         wdn_ref, fs0_ref, fs1_ref, y_ref, g_ref, acc_ref) = refs
    else:
        (pt_ref, x_ref, at_ref, sin_ref, ga_ref, gb_ref, wa_ref, wc_ref, wo_ref, n2_ref, wup_ref, fcw_ref,
         wdn_ref, qs_ref, ck_ref, y_ref, fst_ref, sc_ref, acc_ref, gcar_ref, pbuf, psem) = refs
        first_page, pages_per_step = stream
        page0 = first_page + pl.program_id(0) * pages_per_step
        for i in range(pages_per_step):
            pltpu.make_async_copy(ck_ref.at[0, pt_ref[page0 + i]], pbuf.at[i], psem.at[0]).start()

    a = _dot(at_ref[...].astype(BF16), wa_ref[...])
    b = _dot(sin_ref[...], wc_ref[...])
    merged = _sigmoid(ga_ref[...].astype(F32)) * a + _sigmoid(gb_ref[...].astype(F32)) * b
    x1 = x_ref[...] + _dot(merged.astype(BF16), wo_ref[...])
    ms = jnp.mean(x1 * x1, axis=-1, keepdims=True)
    xn2 = (x1 * lax.rsqrt(ms + EPS) * n2_ref[...]).astype(BF16)
    acc_ref[...] = x1

    if not sample:
        @pl.when(pl.program_id(0) % tiles_per_seq == 0)
        def _():
            gcar_ref[...] = jnp.zeros_like(gcar_ref)

    for c in range(N_FF_CHUNKS):
        cols = slice(c * FF_CHUNK, (c + 1) * FF_CHUNK)
        g = _dot(xn2, wup_ref[:, cols])
        u = _dot(xn2, wup_ref[:, D_FF + c * FF_CHUNK:D_FF + (c + 1) * FF_CHUNK])
        w0, w1, w2 = fcw_ref[0:1, cols], fcw_ref[1:2, cols], fcw_ref[2:3, cols]
        if sample:
            gc = fs0_ref[:, cols] * w0 + fs1_ref[:, cols] * w1 + g * w2
            g_ref[:, cols] = g
        else:
            ge = jnp.concatenate([gcar_ref[:, cols], g], axis=0)
            gc = ge[6:6 + tm] * w0 + ge[7:7 + tm] * w1 + g * w2
            gcar_ref[:, cols] = g[tm - 8:tm]
            fst_ref[0, :, cols] = g[tm - 2:tm]
        hidden = (gc * _sigmoid(gc) * u).astype(BF16)
        acc_ref[...] += _dot(hidden, wdn_ref[cols, :])

        if not sample and c == N_FF_CHUNKS // 2:
            pltpu.make_async_copy(ck_ref.at[0, pl.ds(0, pages_per_step)], pbuf, psem.at[0]).wait()
            q2s = _block_diag_query(qs_ref[0])
            for i in range(pages_per_step):
                sc_ref[i] = _page_scores(q2s, pbuf[i])
    y_ref[...] = acc_ref[...]


def _merge_ffn(x, attn, s_in, ga, gb, wa, wc, wo, n2, wup, fcw, wdn, states, *, sample, seq_len=1, stream=None):
    n_tok = x.shape[0]
    tm = n_tok if sample else TOKEN_TILE
    n_tiles = n_tok // tm
    tiles_per_seq = 1 if sample else seq_len // tm

    def row(width):
        return pl.BlockSpec((tm, width), lambda i, *_: (i, 0))

    in_specs = [row(D_MODEL), row(ATTN_W), row(CONV_CH), row(D_MODEL), row(D_MODEL),
                _const_spec((ATTN_W, D_MODEL)), _const_spec((CONV_CH, D_MODEL)),
                _const_spec((D_MODEL, D_MODEL)), _const_spec((1, D_MODEL)),
                _const_spec((D_MODEL, 2 * D_FF)), _const_spec((3, D_FF)), _const_spec((D_FF, D_MODEL))]
    args = [x, attn, s_in, ga, gb, wa, wc, wo, n2, wup, fcw, wdn]
    scratch = [pltpu.VMEM((tm, D_MODEL), F32)]
    if sample:
        in_specs += [row(D_FF), row(D_FF)]
        args += list(states)
        out_specs = [row(D_MODEL), row(D_FF)]
        out_shape = [jax.ShapeDtypeStruct((n_tok, D_MODEL), F32),
                     jax.ShapeDtypeStruct((n_tok, D_FF), F32)]
    else:
        n_seq = n_tiles // tiles_per_seq
        page_table_flat, q_sample, cache_kt, n_pages, first_page = stream
        pps = FFN_STREAM_PAGES
        assert first_page % n_pages == 0 and n_pages % pps == 0
        in_specs += [pl.BlockSpec((1, 1, ATTN_W), lambda i, *_: ((first_page + i * pps) // n_pages, 0, 0)),
                     pl.BlockSpec(memory_space=pl.ANY)]
        args = [page_table_flat] + args + [q_sample, cache_kt]
        out_specs = [row(D_MODEL), pl.BlockSpec((1, 2, D_FF), lambda i, *_: (i // tiles_per_seq, 0, 0)),
                     pl.BlockSpec((pps, N_HEADS, PAGE), lambda i, *_: (i, 0, 0))]
        out_shape = [jax.ShapeDtypeStruct((n_tok, D_MODEL), F32),
                     jax.ShapeDtypeStruct((n_seq, 2, D_FF), F32),
                     jax.ShapeDtypeStruct((n_tiles * pps, N_HEADS, PAGE), F32)]
        scratch += [pltpu.VMEM((8, D_FF), F32),
                    pltpu.VMEM((pps, N_HEADS, HEAD_DIM, PAGE), F32),
                    pltpu.SemaphoreType.DMA((1,))]
    return pl.pallas_call(
        functools.partial(_merge_ffn_kernel, tm=tm, sample=sample, tiles_per_seq=tiles_per_seq,
                          stream=None if sample else (stream[4], FFN_STREAM_PAGES)),
        grid_spec=pltpu.PrefetchScalarGridSpec(
            num_scalar_prefetch=0 if sample else 1,
            grid=(n_tiles,),
            in_specs=in_specs,
            out_specs=out_specs,
            scratch_shapes=scratch,
        ),
        out_shape=out_shape,
        compiler_params=pltpu.CompilerParams(dimension_semantics=("arbitrary",),
                                             vmem_limit_bytes=VMEM_LIMIT),
    )(*args)


SEQ_PER_SELECT_STEP = 8


def _select_kernel(sc_ref, idx_ref):
    n_pages = sc_ref.shape[1]
    pages_per_block = BLOCK // PAGE
    n_blocks = n_pages // pages_per_block
    blk = lax.broadcasted_iota(jnp.int32, (n_blocks, N_HEADS, 128), 0)
    for s in range(SEQ_PER_SELECT_STEP):
        page_sum = jnp.sum(sc_ref[s], axis=-1, keepdims=True)
        block_sum = jnp.sum(page_sum.reshape(n_blocks, pages_per_block, N_HEADS, 1), axis=1)
        gate = jnp.broadcast_to(block_sum * (1.0 / BLOCK), (n_blocks, N_HEADS, 128))
        for r in range(TOP_K):
            best = jnp.max(gate, axis=0, keepdims=True)
            idx = jnp.min(jnp.where(gate == best, blk, n_blocks), axis=0, keepdims=True)
            idx_ref[s, r] = idx[0]
            gate = jnp.where(blk == idx, -jnp.inf, gate)


def _select_blocks(scores):
    n_seq, n_pages = scores.shape[:2]
    assert n_seq % SEQ_PER_SELECT_STEP == 0
    return pl.pallas_call(
        _select_kernel,
        grid=(n_seq // SEQ_PER_SELECT_STEP,),
        in_specs=[pl.BlockSpec((SEQ_PER_SELECT_STEP, n_pages, N_HEADS, PAGE), lambda i: (i, 0, 0, 0))],
        out_specs=pl.BlockSpec((SEQ_PER_SELECT_STEP, TOP_K, N_HEADS, 128), lambda i: (i, 0, 0, 0)),
        out_shape=jax.ShapeDtypeStruct((n_seq, TOP_K, N_HEADS, 128), jnp.int32),
    )(scores)


PAGES_PER_BLOCK = BLOCK // PAGE
CHUNKS_PER_HEAD = TOP_K * PAGES_PER_BLOCK
N_CHUNKS = N_HEADS * CHUNKS_PER_HEAD
KEYS_PER_HEAD = CHUNKS_PER_HEAD * PAGE


def _sample_attn_kernel(pt_ref, idx_ref, sc0_ref, sc1_ref, q_ref, kn_ref, vn_ref, brow_ref, ownb_ref,
                        cv_ref, o_ref, vbuf, logit_ref, sem, *, n_seq, n_pages, n_seq_first):
    b = pl.program_id(0)
    slot = b % 2

    def score_tile(logical):
        return jnp.where(b < n_seq_first, sc0_ref[0, logical], sc1_ref[0, logical])

    def picked_page(seq, h, r, j):
        n = idx_ref[(seq * TOP_K + r) * N_HEADS + h]
        return n, n * PAGES_PER_BLOCK + j

    def copies(seq, slot_):
        out = []
        for h in range(N_HEADS):
            for r in range(TOP_K):
                for j in range(PAGES_PER_BLOCK):
                    _, logical = picked_page(seq, h, r, j)
                    page = pt_ref[seq * n_pages + logical]
                    c = (h * TOP_K + r) * PAGES_PER_BLOCK + j
                    out.append(pltpu.make_async_copy(cv_ref.at[0, page, h], vbuf.at[slot_, c], sem.at[slot_]))
        return out

    @pl.when(b == 0)
    def _():
        for c in copies(0, 0):
            c.start()

    @pl.when(b + 1 < n_seq)
    def _():
        for c in copies(b + 1, 1 - slot):
            c.start()

    last_block = n_pages // PAGES_PER_BLOCK - 1
    for h in range(N_HEADS):
        for r in range(TOP_K):
            for j in range(PAGES_PER_BLOCK):
                n, logical = picked_page(b, h, r, j)
                piece = score_tile(logical)[h:h + 1, :]
                near = brow_ref[h][:, j * PAGE:(j + 1) * PAGE]
                piece = piece + jnp.where(n == last_block, near, jnp.zeros_like(near))
                c = r * PAGES_PER_BLOCK + j
                logit_ref[h:h + 1, c * PAGE:(c + 1) * PAGE] = piece
    logits = logit_ref[...]

    q = q_ref[0]
    own = jnp.sum(q * kn_ref[0], axis=-1, keepdims=True) + ownb_ref[:, 0, 0:1]
    m = jnp.maximum(jnp.max(logits, axis=-1, keepdims=True), own)
    p = jnp.exp(logits - m)
    p_own = jnp.exp(own - m)
    denom = jnp.sum(p, axis=-1, keepdims=True) + p_own

    p_wide = jnp.concatenate([p] * N_HEADS, axis=1)
    col = lax.broadcasted_iota(jnp.int32, p_wide.shape, 1)
    row = lax.broadcasted_iota(jnp.int32, p_wide.shape, 0)
    p_wide = jnp.where(col // KEYS_PER_HEAD == row, p_wide, 0.0)
    p16 = jnp.concatenate([p_wide, jnp.zeros_like(p_wide)], axis=0).astype(BF16)

    pltpu.make_async_copy(cv_ref.at[0, pl.ds(0, N_CHUNKS), 0], vbuf.at[slot], sem.at[slot]).wait()
    vt = jnp.concatenate([vbuf[slot, c] for c in range(N_CHUNKS)], axis=1).astype(BF16)
    ctx = _dot_nt(p16, vt)[:N_HEADS] + p_own * vn_ref[0]
    o_ref[0] = ctx / denom


def _sample_attention(page_table_flat, idx_flat, score_parts, q_heads, k_new, v_new, brow, ownb, cache_vt, n_pages):
    n_seq = q_heads.shape[0]
    head_spec = pl.BlockSpec((1, N_HEADS, HEAD_DIM), lambda b, pt, ix: (b, 0, 0))
    starts = [0, score_parts[0].shape[0]]

    def part_spec(k):
        last = score_parts[k].shape[0] - 1
        return pl.BlockSpec((1, n_pages, N_HEADS, PAGE),
                            lambda b, pt, ix: (jnp.clip(b - starts[k], 0, last), 0, 0, 0))

    return pl.pallas_call(
        functools.partial(_sample_attn_kernel, n_seq=n_seq, n_pages=n_pages, n_seq_first=starts[1]),
        grid_spec=pltpu.PrefetchScalarGridSpec(
            num_scalar_prefetch=2,
            grid=(n_seq,),
            in_specs=[part_spec(0), part_spec(1),
                      head_spec, head_spec, head_spec,
                      pl.BlockSpec((N_HEADS, 1, BLOCK), lambda b, pt, ix: (0, 0, 0)),
                      pl.BlockSpec((N_HEADS, 1, 128), lambda b, pt, ix: (0, 0, 0)),
                      pl.BlockSpec(memory_space=pl.ANY)],
            out_specs=head_spec,
            scratch_shapes=[pltpu.VMEM((2, N_CHUNKS, HEAD_DIM, PAGE), F32),
                            pltpu.VMEM((N_HEADS, KEYS_PER_HEAD), F32),
                            pltpu.SemaphoreType.DMA((2,))],
        ),
        out_shape=jax.ShapeDtypeStruct((n_seq, N_HEADS, HEAD_DIM), F32),
        compiler_params=pltpu.CompilerParams(dimension_semantics=("arbitrary",),
                                             vmem_limit_bytes=VMEM_LIMIT),
    )(page_table_flat, idx_flat, *score_parts, q_heads, k_new, v_new, brow, ownb, cache_vt)


def kernel(x_prompt, x_sample, cache_k, cache_v, page_table, state_conv, state_ffn, norm1_w, w_in, q_norm_w, k_norm_w, conv_w, w_attn_up, w_conv_out, w_o, norm2_w, w_ffn_up, ffn_conv_w, w_ffn_down, rel_bias):
    batch, seq, _ = x_prompt.shape
    n_seq = x_sample.shape[0]
    n_pages = page_table.shape[1]

    w_in_b = w_in[0].astype(BF16)
    wa = w_attn_up[0].astype(BF16)
    wc = w_conv_out[0].astype(BF16)
    wo = w_o[0].astype(BF16)
    wup = w_ffn_up[0].astype(BF16)
    wdn = w_ffn_down[0].astype(BF16)
    fcw = ffn_conv_w[0]
    n1 = norm1_w[0].reshape(1, D_MODEL)
    n2 = norm2_w[0].reshape(1, D_MODEL)
    qn = jnp.tile(q_norm_w[0], N_HEADS).reshape(1, ATTN_W)
    kn = jnp.tile(k_norm_w[0], N_HEADS).reshape(1, ATTN_W)
    lane = np.arange(BLOCK)
    bd = jnp.asarray((lane[:, None] // HEAD_DIM == lane[None, :] // HEAD_DIM) / HEAD_DIM, BF16)
    cw = conv_w[0]

    bias_tiles, bias_row, bias_own = _bias_tables(rel_bias)

    xs = x_sample.reshape(n_seq, D_MODEL)
    (q_s, k_s, v_s, sin_s, ga_s, gb_s, u_s) = _inproj(
        xs, n1, w_in_b, qn, kn, bd, cw, (state_conv[0, :, 0, :], state_conv[0, :, 1, :]), sample=True)
    pt_flat = page_table.reshape(-1)
    cache_kt = cache_k.transpose(0, 1, 3, 4, 2)
    cache_vt = cache_v.transpose(0, 1, 3, 4, 2)
    xp = x_prompt.reshape(batch * seq, D_MODEL)
    n_attn_pages = n_seq * n_pages - (batch * seq // TOKEN_TILE) * FFN_STREAM_PAGES
    (q_b, kt_p, vt_p, k_b, vt_b, ksum_p, sin_p, ga_p, gb_p, conv_p) = _inproj(
        xp, n1, w_in_b, qn, kn, bd, cw, None, sample=False, seq_len=seq)

    attn_p, scores_0 = _prompt_attention(q_b.reshape(batch, seq, ATTN_W), k_b.reshape(batch, seq, ATTN_W), vt_b,
                                         ksum_p.reshape(batch, seq // BLOCK, ATTN_W), bias_tiles,
                                         pt_flat, q_s, cache_kt, n_pages, n_attn_pages)
    y_p, ffn_p, scores_1 = _merge_ffn(xp, attn_p.reshape(batch * seq, ATTN_W), sin_p, ga_p, gb_p,
                                      wa, wc, wo, n2, wup, fcw, wdn, None, sample=False, seq_len=seq,
                                      stream=(pt_flat, q_s.reshape(n_seq, 1, ATTN_W), cache_kt, n_pages, n_attn_pages))

    score_parts = [sc.reshape(-1, n_pages, N_HEADS, PAGE) for sc in (scores_0, scores_1)]
    idx = jnp.concatenate([_select_blocks(sc)[..., 0] for sc in score_parts], axis=0)
    attn_s = _sample_attention(pt_flat, idx.reshape(-1), score_parts, q_s.reshape(n_seq, N_HEADS, HEAD_DIM),
                               k_s.reshape(n_seq, N_HEADS, HEAD_DIM), v_s.reshape(n_seq, N_HEADS, HEAD_DIM),
                               bias_row, bias_own, cache_vt, n_pages)

    y_s, g_s = _merge_ffn(xs, attn_s.reshape(n_seq, ATTN_W), sin_s, ga_s, gb_s,
                          wa, wc, wo, n2, wup, fcw, wdn,
                          (state_ffn[0, :, 0, :], state_ffn[0, :, 1, :]), sample=True)

    return (
        y_p.reshape(batch, seq, D_MODEL),
        y_s.reshape(n_seq, 1, D_MODEL),
        kt_p.reshape(1, batch, N_HEADS, HEAD_DIM, seq).transpose(0, 1, 4, 2, 3),
        vt_p.reshape(1, batch, N_HEADS, HEAD_DIM, seq).transpose(0, 1, 4, 2, 3),
        conv_p[None],
        ffn_p[None],
        k_s.reshape(1, n_seq, 1, N_HEADS, HEAD_DIM),
        v_s.reshape(1, n_seq, 1, N_HEADS, HEAD_DIM),
        jnp.stack([state_conv[0, :, 1, :], u_s], axis=1)[None],
        jnp.stack([state_ffn[0, :, 1, :], g_s], axis=1)[None],
    )
```

```python
import functools
import math

import numpy as np
import jax
import jax.numpy as jnp
from jax import lax
from jax.experimental import pallas as pl
from jax.experimental.pallas import tpu as pltpu

F32 = jnp.float32
BF16 = jnp.bfloat16

D_MODEL = 1024
N_HEADS = 8
HEAD_DIM = 64
ATTN_W = N_HEADS * HEAD_DIM
CONV_CH = D_MODEL // 2
D_FF = 2816
BLOCK = 256
TOP_K = 3
PAGE = 128
NUM_BUCKETS = 32
MAX_DISTANCE = 128
EPS = 1e-6
N_COLS = 3 * ATTN_W + 3 * CONV_CH + 2 * D_MODEL

FF_CHUNK = 256
N_FF_CHUNKS = D_FF // FF_CHUNK
TOKEN_TILE = 512
NEG_BIG = -1e30
VMEM_LIMIT = 60 * 1024 * 1024
LOG2E = math.log2(math.e)


def _bucket_thresholds():
    max_exact = NUM_BUCKETS // 2
    d = np.arange(0, 4 * MAX_DISTANCE)
    logd = np.log(np.maximum(d, 1) / max_exact)
    large = max_exact + (logd / math.log(MAX_DISTANCE / max_exact) * (NUM_BUCKETS - max_exact)).astype(np.int32)
    bucket = np.where(d < max_exact, d, np.minimum(large, NUM_BUCKETS - 1))
    return [int(np.argmax(bucket >= b)) for b in range(NUM_BUCKETS)]


BUCKET_THR = _bucket_thresholds()


def _dot(a, b):
    return jnp.dot(a, b, preferred_element_type=F32)


def _dot_nt(a, b):
    return lax.dot_general(a, b, (((1,), (1,)), ((), ())), preferred_element_type=F32)


def _sigmoid(x):
    return 0.5 * jnp.tanh(0.5 * x) + 0.5


def _const_spec(shape):
    nd = len(shape)
    return pl.BlockSpec(shape, lambda *_: (0,) * nd, pipeline_mode=pl.Buffered(1))


def _bias_from_distance(d, rb_ref, h, b_far):
    val = jnp.zeros(d.shape, F32)
    for b in range(NUM_BUCKETS - 2, -1, -1):
        val = jnp.where(d < BUCKET_THR[b + 1], rb_ref[b, h] - b_far, val)
    return val


def _bias_kernel(rb_ref, tile_ref, row_ref, own_ref):
    hp = pl.program_id(0)
    j = lax.broadcasted_iota(jnp.int32, (BLOCK, BLOCK), 0)
    i = lax.broadcasted_iota(jnp.int32, (BLOCK, BLOCK), 1)
    jr = lax.broadcasted_iota(jnp.int32, (1, BLOCK), 1)
    for e in range(2):
        h = 2 * hp + e
        b_far = rb_ref[NUM_BUCKETS - 1, h]
        lanes = slice(e * BLOCK, (e + 1) * BLOCK)
        tile_ref[0, 0, :, lanes] = jnp.where(j <= i, _bias_from_distance(i - j, rb_ref, h, b_far) * LOG2E, NEG_BIG)
        tile_ref[0, 1, :, lanes] = _bias_from_distance(i - j + BLOCK, rb_ref, h, b_far) * LOG2E
        row_ref[e] = _bias_from_distance(BLOCK - jr, rb_ref, h, b_far)
        own_ref[e] = jnp.zeros((1, 128), F32) + (rb_ref[0, h] - b_far)
    tile_ref[0, 2] = jnp.zeros((BLOCK, 2 * BLOCK), F32)


def _bias_tables(rel_bias):
    n_hp = N_HEADS // 2
    return pl.pallas_call(
        _bias_kernel,
        grid=(n_hp,),
        in_specs=[pl.BlockSpec(memory_space=pltpu.SMEM)],
        out_specs=[
            pl.BlockSpec((1, 3, BLOCK, 2 * BLOCK), lambda hp: (hp, 0, 0, 0)),
            pl.BlockSpec((2, 1, BLOCK), lambda hp: (hp, 0, 0)),
            pl.BlockSpec((2, 1, 128), lambda hp: (hp, 0, 0)),
        ],
        out_shape=[
            jax.ShapeDtypeStruct((n_hp, 3, BLOCK, 2 * BLOCK), F32),
            jax.ShapeDtypeStruct((N_HEADS, 1, BLOCK), F32),
            jax.ShapeDtypeStruct((N_HEADS, 1, 128), F32),
        ],
    )(rel_bias)


def _inproj_kernel(*refs, tm, sample, tiles_per_seq):
    if sample:
        (x_ref, n1_ref, w_ref, qn_ref, kn_ref, bd_ref, cw_ref, st0_ref, st1_ref,
         q_ref, k_ref, v_ref, sin_ref, ga_ref, gb_ref, u_ref) = refs
    else:
        (x_ref, n1_ref, w_ref, qn_ref, kn_ref, bd_ref, cw_ref,
         q_ref, kt_ref, vt_ref, kb_ref, vtb_ref, ksum_ref, sin_ref, ga_ref, gb_ref, cst_ref,
         ucar_ref) = refs

    x = x_ref[...]
    ms = jnp.mean(x * x, axis=-1, keepdims=True)
    xn = (x * lax.rsqrt(ms + EPS) * n1_ref[...]).astype(BF16)

    def proj(a, b):
        return _dot(xn, w_ref[:, a:b])

    def head_norm(t, w_row):
        sq = (t * t).astype(BF16)
        bd = bd_ref[...]
        half = ATTN_W // 2
        msq = jnp.concatenate([_dot(sq[:, :half], bd), _dot(sq[:, half:], bd)], axis=1)
        return t * lax.rsqrt(msq + EPS) * w_row

    q = head_norm(proj(0, ATTN_W), qn_ref[...]) * (HEAD_DIM ** -0.5)
    k = head_norm(proj(ATTN_W, 2 * ATTN_W), kn_ref[...])
    v = proj(2 * ATTN_W, 3 * ATTN_W)
    c0 = 3 * ATTN_W
    cb = proj(c0, c0 + CONV_CH)
    u = proj(c0 + CONV_CH, c0 + 2 * CONV_CH) * proj(c0 + 2 * CONV_CH, c0 + 3 * CONV_CH)
    g0 = c0 + 3 * CONV_CH
    ga_ref[...] = proj(g0, g0 + D_MODEL).astype(ga_ref.dtype)
    gb_ref[...] = proj(g0 + D_MODEL, g0 + 2 * D_MODEL).astype(gb_ref.dtype)

    w0, w1, w2 = cw_ref[0:1, :], cw_ref[1:2, :], cw_ref[2:3, :]
    if sample:
        q_ref[...] = q
        k_ref[...] = k
        v_ref[...] = v
        uc = st0_ref[...] * w0 + st1_ref[...] * w1 + u * w2
        u_ref[...] = u
    else:
        q_ref[...] = (q * LOG2E).astype(BF16)
        kb_ref[...] = k.astype(BF16)
        kt_ref[0] = k.T
        v_t = v.T
        vt_ref[0] = v_t
        for r in range(tm // BLOCK):
            vtb_ref[0, r] = v_t[:, r * BLOCK:(r + 1) * BLOCK].astype(BF16)
            ksum_ref[0, r:r + 1, :] = jnp.sum(k[r * BLOCK:(r + 1) * BLOCK], axis=0, keepdims=True)

        @pl.when(pl.program_id(0) % tiles_per_seq == 0)
        def _():
            ucar_ref[...] = jnp.zeros_like(ucar_ref)

        ue = jnp.concatenate([ucar_ref[...], u], axis=0)
        uc = ue[6:6 + tm] * w0 + ue[7:7 + tm] * w1 + u * w2
        ucar_ref[...] = u[tm - 8:tm]
        cst_ref[0] = u[tm - 2:tm]
    sin_ref[...] = (cb * uc).astype(sin_ref.dtype)


def _inproj(x, n1, w_in, qn, kn, bd, conv_w, states, *, sample, seq_len=1):
    n_tok = x.shape[0]
    tm = n_tok if sample else TOKEN_TILE
    n_tiles = n_tok // tm
    tiles_per_seq = 1 if sample else seq_len // tm

    def row(width):
        return pl.BlockSpec((tm, width), lambda i, *_: (i, 0))

    in_specs = [row(D_MODEL), _const_spec((1, D_MODEL)), _const_spec((D_MODEL, N_COLS)),
                _const_spec((1, ATTN_W)), _const_spec((1, ATTN_W)), _const_spec((BLOCK, BLOCK)),
                _const_spec((3, CONV_CH))]
    args = [x, n1, w_in, qn, kn, bd, conv_w]
    if sample:
        in_specs += [row(CONV_CH), row(CONV_CH)]
        args += list(states)
        out_specs = [row(ATTN_W), row(ATTN_W), row(ATTN_W), row(CONV_CH), row(D_MODEL), row(D_MODEL),
                     row(CONV_CH)]
        out_shape = [jax.ShapeDtypeStruct((n_tok, ATTN_W), F32)] * 3 + [
            jax.ShapeDtypeStruct((n_tok, CONV_CH), BF16),
            jax.ShapeDtypeStruct((n_tok, D_MODEL), BF16),
            jax.ShapeDtypeStruct((n_tok, D_MODEL), BF16),
            jax.ShapeDtypeStruct((n_tok, CONV_CH), F32)]
        scratch = []
    else:
        blocks_per_tile = tm // BLOCK
        n_seq = n_tiles // tiles_per_seq
        t_spec = pl.BlockSpec((1, ATTN_W, tm), lambda i, *_: (i // tiles_per_seq, 0, i % tiles_per_seq))
        out_specs = [row(ATTN_W), t_spec, t_spec, row(ATTN_W),
                     pl.BlockSpec((1, blocks_per_tile, ATTN_W, BLOCK),
                                  lambda i, *_: (i // tiles_per_seq, i % tiles_per_seq, 0, 0)),
                     pl.BlockSpec((1, blocks_per_tile, ATTN_W), lambda i, *_: (i, 0, 0)),
                     row(CONV_CH), row(D_MODEL), row(D_MODEL),
                     pl.BlockSpec((1, 2, CONV_CH), lambda i, *_: (i // tiles_per_seq, 0, 0))]
        out_shape = [jax.ShapeDtypeStruct((n_tok, ATTN_W), BF16),
                     jax.ShapeDtypeStruct((n_seq, ATTN_W, seq_len), F32),
                     jax.ShapeDtypeStruct((n_seq, ATTN_W, seq_len), F32),
                     jax.ShapeDtypeStruct((n_tok, ATTN_W), BF16),
                     jax.ShapeDtypeStruct((n_seq, seq_len // BLOCK, ATTN_W, BLOCK), BF16),
                     jax.ShapeDtypeStruct((n_tiles, blocks_per_tile, ATTN_W), F32),
                     jax.ShapeDtypeStruct((n_tok, CONV_CH), BF16),
                     jax.ShapeDtypeStruct((n_tok, D_MODEL), BF16),
                     jax.ShapeDtypeStruct((n_tok, D_MODEL), BF16),
                     jax.ShapeDtypeStruct((n_tiles // tiles_per_seq, 2, CONV_CH), F32)]
        scratch = [pltpu.VMEM((8, CONV_CH), F32)]
    return pl.pallas_call(
        functools.partial(_inproj_kernel, tm=tm, sample=sample, tiles_per_seq=tiles_per_seq),
        grid=(n_tiles,),
        in_specs=in_specs,
        out_specs=out_specs,
        scratch_shapes=scratch,
        out_shape=out_shape,
        compiler_params=pltpu.CompilerParams(dimension_semantics=("arbitrary",),
                                             vmem_limit_bytes=VMEM_LIMIT),
    )(*args)


STREAM_SLOTS = 5


def _query_block_order(t, n_blocks):
    return jnp.where(t % 2 == 0, t // 2, n_blocks - 1 - t // 2)
FFN_STREAM_PAGES = 32


def _block_diag_query(q_row):
    row = lax.broadcasted_iota(jnp.int32, (N_HEADS, ATTN_W), 0)
    lane_head = lax.broadcasted_iota(jnp.int32, (N_HEADS, ATTN_W), 1) // HEAD_DIM
    q_rows = jnp.where(row == lane_head, q_row, 0.0)
    q_hi = q_rows.astype(BF16).astype(F32)
    return jnp.concatenate([q_hi, q_rows - q_hi], axis=0).astype(BF16)


def _page_scores(q2, page):
    s2 = _dot(q2, page.reshape(ATTN_W, PAGE).astype(BF16))
    return s2[:N_HEADS] + s2[N_HEADS:]


def _attn_kernel(pt_ref, q_ref, k_ref, vt_ref, ksum_ref, bias_ref, qs_ref, ck_ref, o_ref, sc_ref,
                 pen_ref, m_ref, l_ref, alpha_ref, acc_ref, s_ref, p_ref, pbuf, psem, *, pages_per_step, n_pages):
    n_blocks = k_ref.shape[1] // BLOCK
    qb = _query_block_order(pl.program_id(2), n_blocks)
    nq = 2 * BLOCK
    step = (pl.program_id(0) * pl.num_programs(1) + pl.program_id(1)) * pl.num_programs(2) + pl.program_id(2)
    n_steps = pl.num_programs(0) * pl.num_programs(1) * pl.num_programs(2)

    def page_copies(step_, slot_):
        return [pltpu.make_async_copy(ck_ref.at[0, pt_ref[step_ * pages_per_step + i]], pbuf.at[slot_, i],
                                      psem.at[slot_]) for i in range(pages_per_step)]

    @pl.when(step == 0)
    def _():
        for ahead in range(STREAM_SLOTS - 1):
            for c in page_copies(ahead, ahead):
                c.start()

    @pl.when(step + (STREAM_SLOTS - 1) < n_steps)
    def _():
        nxt = step + (STREAM_SLOTS - 1)
        for c in page_copies(nxt, nxt % STREAM_SLOTS):
            c.start()

    slot = step % STREAM_SLOTS
    pltpu.make_async_copy(ck_ref.at[0, pl.ds(0, pages_per_step)], pbuf.at[slot], psem.at[slot]).wait()
    page0 = step * pages_per_step
    seq0 = page0 // n_pages
    seq1 = jnp.minimum(seq0 + 1, qs_ref.shape[0] - 1)
    q2s0 = _block_diag_query(qs_ref[pl.ds(seq0, 1), :])
    q2s1 = _block_diag_query(qs_ref[pl.ds(seq1, 1), :])

    def score_page(i):
        q2s = jnp.where((page0 + i) // n_pages == seq0, q2s0, q2s1)
        sc_ref[i] = _page_scores(q2s, pbuf[slot, i])

    pages_at_head = (pages_per_step * 5) // 8
    for i in range(pages_at_head):
        score_page(i)

    q = q_ref[0]
    lane_head = lax.broadcasted_iota(jnp.int32, q.shape, 1) // HEAD_DIM
    zero = jnp.zeros_like(q)
    q2 = jnp.concatenate([jnp.where(lane_head == 0, q, zero), jnp.where(lane_head == 1, q, zero)], axis=0)

    kmean = ksum_ref[0] * (1.0 / BLOCK)
    km_hi = kmean.astype(BF16)
    km_lo = (kmean - km_hi.astype(F32)).astype(BF16)
    g2 = _dot_nt(jnp.concatenate([km_hi, km_lo], axis=0), q2)
    g = g2[:n_blocks] + g2[n_blocks:]
    blk = lax.broadcasted_iota(jnp.int32, (n_blocks, nq), 0)
    valid = blk < qb
    g = jnp.where(valid, g, -jnp.inf)
    rank = jnp.zeros(g.shape, jnp.int32)
    for j in range(n_blocks):
        gj = g[j:j + 1, :]
        beats = (gj > g) | ((gj == g) & (blk > j))
        rank = rank + beats.astype(jnp.int32)
    sel = valid & (rank < TOP_K)
    pen_ref[0:n_blocks] = jnp.where(sel, 0.0, NEG_BIG)
    pen_ref[n_blocks:2 * n_blocks] = jnp.zeros((n_blocks, nq), F32)

    def item(j):
        own = j == 0
        kb = jnp.where(own, qb, j - 1)
        tile = jnp.where(own, 0, jnp.where(j == qb, 1, 2))
        pen_row = jnp.where(own, n_blocks, j - 1)
        return kb, tile, pen_row

    def item_scores(j):
        kb, tile, pen_row = item(j)
        kblk = k_ref[0, pl.ds(pl.multiple_of(kb * BLOCK, BLOCK), BLOCK), :]
        return _dot_nt(kblk, q2) + pen_ref[pl.ds(pen_row, 1), :] + bias_ref[0, tile]

    def pv(kb, p):
        vt = vt_ref[0, kb]
        return jnp.concatenate([_dot(vt[:HEAD_DIM], p[:, :BLOCK]), _dot(vt[HEAD_DIM:], p[:, BLOCK:])], axis=1)

    n_items = qb + 1
    s_ref[...] = item_scores(0)
    m_ref[...] = jnp.full(m_ref.shape, NEG_BIG, F32)
    l_ref[...] = jnp.zeros(l_ref.shape, F32)
    alpha_ref[...] = jnp.ones(alpha_ref.shape, F32)
    acc_ref[...] = jnp.zeros(acc_ref.shape, F32)
    p_ref[...] = jnp.zeros(p_ref.shape, BF16)

    def trip(j, carry):
        s = s_ref[...]
        s_next = item_scores(jnp.minimum(j + 1, n_items - 1))
        kb_prev, _, _ = item(jnp.maximum(j - 1, 0))
        acc = alpha_ref[...] * acc_ref[...] + pv(kb_prev, p_ref[...])
        m_old = m_ref[...]
        m_new = jnp.maximum(m_old, jnp.max(s, axis=0, keepdims=True))
        alpha = jnp.exp2(m_old - m_new)
        p = jnp.exp2(s - m_new)
        l_ref[...] = alpha * l_ref[...] + jnp.sum(p, axis=0, keepdims=True)
        m_ref[...] = m_new
        acc_ref[...] = acc
        alpha_ref[...] = alpha
        p_ref[...] = p.astype(BF16)
        s_ref[...] = s_next
        return carry

    lax.fori_loop(0, n_items, trip, 0)
    kb_last, _, _ = item(qb)
    acc = alpha_ref[...] * acc_ref[...] + pv(kb_last, p_ref[...])
    out = acc / l_ref[...]
    out_t = jnp.concatenate([out[:, :BLOCK], out[:, BLOCK:]], axis=0)
    o_ref[0] = out_t.T.astype(o_ref.dtype)

    for i in range(pages_at_head, pages_per_step):
        score_page(i)


def _prompt_attention(q, k, vt_blocks, ksum, bias_tiles, page_table_flat, q_sample, cache_kt, n_pages,
                      n_stream_pages):
    b, s, _ = q.shape
    n_blocks = s // BLOCK
    n_hp = N_HEADS // 2
    n_steps = b * n_hp * n_blocks
    n_seq = q_sample.shape[0]
    pages_per_step = n_stream_pages // n_steps
    assert pages_per_step * n_steps == n_stream_pages and pages_per_step <= n_pages

    def step_of(bi, hp, qb):
        return (bi * n_hp + hp) * n_blocks + qb

    attn, scores = pl.pallas_call(
        functools.partial(_attn_kernel, pages_per_step=pages_per_step, n_pages=n_pages),
        grid_spec=pltpu.PrefetchScalarGridSpec(
            num_scalar_prefetch=1,
            grid=(b, n_hp, n_blocks),
            in_specs=[
                pl.BlockSpec((1, BLOCK, 128), lambda bi, hp, qb, pt: (bi, _query_block_order(qb, n_blocks), hp)),
                pl.BlockSpec((1, s, 128), lambda bi, hp, qb, pt: (bi, 0, hp)),
                pl.BlockSpec((1, n_blocks, 128, BLOCK), lambda bi, hp, qb, pt: (bi, 0, hp, 0)),
                pl.BlockSpec((1, n_blocks, 128), lambda bi, hp, qb, pt: (bi, 0, hp)),
                pl.BlockSpec((1, 3, BLOCK, 2 * BLOCK), lambda bi, hp, qb, pt: (hp, 0, 0, 0)),
                pl.BlockSpec((n_seq, ATTN_W), lambda bi, hp, qb, pt: (0, 0), pipeline_mode=pl.Buffered(1)),
                pl.BlockSpec(memory_space=pl.ANY),
            ],
            out_specs=[
                pl.BlockSpec((1, BLOCK, 128), lambda bi, hp, qb, pt: (bi, _query_block_order(qb, n_blocks), hp)),
                pl.BlockSpec((pages_per_step, N_HEADS, PAGE), lambda bi, hp, qb, pt: (step_of(bi, hp, qb), 0, 0)),
            ],
            scratch_shapes=[
                pltpu.VMEM((2 * n_blocks, 2 * BLOCK), F32),
                pltpu.VMEM((1, 2 * BLOCK), F32),
                pltpu.VMEM((1, 2 * BLOCK), F32),
                pltpu.VMEM((1, 2 * BLOCK), F32),
                pltpu.VMEM((HEAD_DIM, 2 * BLOCK), F32),
                pltpu.VMEM((BLOCK, 2 * BLOCK), F32),
                pltpu.VMEM((BLOCK, 2 * BLOCK), BF16),
                pltpu.VMEM((STREAM_SLOTS, pages_per_step, N_HEADS, HEAD_DIM, PAGE), F32),
                pltpu.SemaphoreType.DMA((STREAM_SLOTS,)),
            ],
        ),
        out_shape=[jax.ShapeDtypeStruct((b, s, ATTN_W), BF16),
                   jax.ShapeDtypeStruct((n_stream_pages, N_HEADS, PAGE), F32)],
        compiler_params=pltpu.CompilerParams(
            dimension_semantics=("arbitrary", "arbitrary", "arbitrary"),
            vmem_limit_bytes=VMEM_LIMIT),
    )(page_table_flat, q, k, vt_blocks, ksum, bias_tiles, q_sample, cache_kt)
    return attn, scores


def _merge_ffn_kernel(*refs, tm, sample, tiles_per_seq, stream):
    if sample:
        (x_ref, at_ref, sin_ref, ga_ref, gb_ref, wa_ref, wc_ref, wo_ref, n2_ref, wup_ref, fcw_ref,
         wdn_ref, fs0_ref, fs1_ref, y_ref, g_ref, acc_ref) = refs
    else:
        (pt_ref, x_ref, at_ref, sin_ref, ga_ref, gb_ref, wa_ref, wc_ref, wo_ref, n2_ref, wup_ref, fcw_ref,
         wdn_ref, qs_ref, ck_ref, y_ref, fst_ref, sc_ref, acc_ref, gcar_ref, pbuf, psem) = refs
        first_page, pages_per_step = stream
        page0 = first_page + pl.program_id(0) * pages_per_step
        for i in range(pages_per_step):
            pltpu.make_async_copy(ck_ref.at[0, pt_ref[page0 + i]], pbuf.at[i], psem.at[0]).start()

    a = _dot(at_ref[...].astype(BF16), wa_ref[...])
    b = _dot(sin_ref[...], wc_ref[...])
    merged = _sigmoid(ga_ref[...].astype(F32)) * a + _sigmoid(gb_ref[...].astype(F32)) * b
    x1 = x_ref[...] + _dot(merged.astype(BF16), wo_ref[...])
    ms = jnp.mean(x1 * x1, axis=-1, keepdims=True)
    xn2 = (x1 * lax.rsqrt(ms + EPS) * n2_ref[...]).astype(BF16)
    acc_ref[...] = x1

    if not sample:
        @pl.when(pl.program_id(0) % tiles_per_seq == 0)
        def _():
            gcar_ref[...] = jnp.zeros_like(gcar_ref)

    for c in range(N_FF_CHUNKS):
        cols = slice(c * FF_CHUNK, (c + 1) * FF_CHUNK)
        g = _dot(xn2, wup_ref[:, cols])
        u = _dot(xn2, wup_ref[:, D_FF + c * FF_CHUNK:D_FF + (c + 1) * FF_CHUNK])
        w0, w1, w2 = fcw_ref[0:1, cols], fcw_ref[1:2, cols], fcw_ref[2:3, cols]
        if sample:
            gc = fs0_ref[:, cols] * w0 + fs1_ref[:, cols] * w1 + g * w2
            g_ref[:, cols] = g
        else:
            ge = jnp.concatenate([gcar_ref[:, cols], g], axis=0)
            gc = ge[6:6 + tm] * w0 + ge[7:7 + tm] * w1 + g * w2
            gcar_ref[:, cols] = g[tm - 8:tm]
            fst_ref[0, :, cols] = g[tm - 2:tm]
        hidden = (gc * _sigmoid(gc) * u).astype(BF16)
        acc_ref[...] += _dot(hidden, wdn_ref[cols, :])

        if not sample and c == N_FF_CHUNKS // 2:
            pltpu.make_async_copy(ck_ref.at[0, pl.ds(0, pages_per_step)], pbuf, psem.at[0]).wait()
            q2s = _block_diag_query(qs_ref[0])
            for i in range(pages_per_step):
                sc_ref[i] = _page_scores(q2s, pbuf[i])
    y_ref[...] = acc_ref[...]


def _merge_ffn(x, attn, s_in, ga, gb, wa, wc, wo, n2, wup, fcw, wdn, states, *, sample, seq_len=1, stream=None):
    n_tok = x.shape[0]
    tm = n_tok if sample else TOKEN_TILE
    n_tiles = n_tok // tm
    tiles_per_seq = 1 if sample else seq_len // tm

    def row(width):
        return pl.BlockSpec((tm, width), lambda i, *_: (i, 0))

    in_specs = [row(D_MODEL), row(ATTN_W), row(CONV_CH), row(D_MODEL), row(D_MODEL),
                _const_spec((ATTN_W, D_MODEL)), _const_spec((CONV_CH, D_MODEL)),
                _const_spec((D_MODEL, D_MODEL)), _const_spec((1, D_MODEL)),
                _const_spec((D_MODEL, 2 * D_FF)), _const_spec((3, D_FF)), _const_spec((D_FF, D_MODEL))]
    args = [x, attn, s_in, ga, gb, wa, wc, wo, n2, wup, fcw, wdn]
    scratch = [pltpu.VMEM((tm, D_MODEL), F32)]
    if sample:
        in_specs += [row(D_FF), row(D_FF)]
        args += list(states)
        out_specs = [row(D_MODEL), row(D_FF)]
        out_shape = [jax.ShapeDtypeStruct((n_tok, D_MODEL), F32),
                     jax.ShapeDtypeStruct((n_tok, D_FF), F32)]
    else:
        n_seq = n_tiles // tiles_per_seq
        page_table_flat, q_sample, cache_kt, n_pages, first_page = stream
        pps = FFN_STREAM_PAGES
        assert first_page % n_pages == 0 and n_pages % pps == 0
        in_specs += [pl.BlockSpec((1, 1, ATTN_W), lambda i, *_: ((first_page + i * pps) // n_pages, 0, 0)),
                     pl.BlockSpec(memory_space=pl.ANY)]
        args = [page_table_flat] + args + [q_sample, cache_kt]
        out_specs = [row(D_MODEL), pl.BlockSpec((1, 2, D_FF), lambda i, *_: (i // tiles_per_seq, 0, 0)),
                     pl.BlockSpec((pps, N_HEADS, PAGE), lambda i, *_: (i, 0, 0))]
        out_shape = [jax.ShapeDtypeStruct((n_tok, D_MODEL), F32),
                     jax.ShapeDtypeStruct((n_seq, 2, D_FF), F32),
                     jax.ShapeDtypeStruct((n_tiles * pps, N_HEADS, PAGE), F32)]
        scratch += [pltpu.VMEM((8, D_FF), F32),
                    pltpu.VMEM((pps, N_HEADS, HEAD_DIM, PAGE), F32),
                    pltpu.SemaphoreType.DMA((1,))]
    return pl.pallas_call(
        functools.partial(_merge_ffn_kernel, tm=tm, sample=sample, tiles_per_seq=tiles_per_seq,
                          stream=None if sample else (stream[4], FFN_STREAM_PAGES)),
        grid_spec=pltpu.PrefetchScalarGridSpec(
            num_scalar_prefetch=0 if sample else 1,
            grid=(n_tiles,),
            in_specs=in_specs,
            out_specs=out_specs,
            scratch_shapes=scratch,
        ),
        out_shape=out_shape,
        compiler_params=pltpu.CompilerParams(dimension_semantics=("arbitrary",),
                                             vmem_limit_bytes=VMEM_LIMIT),
    )(*args)


SEQ_PER_SELECT_STEP = 8


def _select_kernel(sc_ref, idx_ref):
    n_pages = sc_ref.shape[1]
    pages_per_block = BLOCK // PAGE
    n_blocks = n_pages // pages_per_block
    blk = lax.broadcasted_iota(jnp.int32, (n_blocks, N_HEADS, 128), 0)
    for s in range(SEQ_PER_SELECT_STEP):
        page_sum = jnp.sum(sc_ref[s], axis=-1, keepdims=True)
        block_sum = jnp.sum(page_sum.reshape(n_blocks, pages_per_block, N_HEADS, 1), axis=1)
        gate = jnp.broadcast_to(block_sum * (1.0 / BLOCK), (n_blocks, N_HEADS, 128))
        for r in range(TOP_K):
            best = jnp.max(gate, axis=0, keepdims=True)
            idx = jnp.min(jnp.where(gate == best, blk, n_blocks), axis=0, keepdims=True)
            idx_ref[s, r] = idx[0]
            gate = jnp.where(blk == idx, -jnp.inf, gate)


def _select_blocks(scores):
    n_seq, n_pages = scores.shape[:2]
    assert n_seq % SEQ_PER_SELECT_STEP == 0
    return pl.pallas_call(
        _select_kernel,
        grid=(n_seq // SEQ_PER_SELECT_STEP,),
        in_specs=[pl.BlockSpec((SEQ_PER_SELECT_STEP, n_pages, N_HEADS, PAGE), lambda i: (i, 0, 0, 0))],
        out_specs=pl.BlockSpec((SEQ_PER_SELECT_STEP, TOP_K, N_HEADS, 128), lambda i: (i, 0, 0, 0)),
        out_shape=jax.ShapeDtypeStruct((n_seq, TOP_K, N_HEADS, 128), jnp.int32),
    )(scores)


PAGES_PER_BLOCK = BLOCK // PAGE
CHUNKS_PER_HEAD = TOP_K * PAGES_PER_BLOCK
N_CHUNKS = N_HEADS * CHUNKS_PER_HEAD
KEYS_PER_HEAD = CHUNKS_PER_HEAD * PAGE


def _sample_attn_kernel(pt_ref, idx_ref, sc0_ref, sc1_ref, q_ref, kn_ref, vn_ref, brow_ref, ownb_ref,
                        cv_ref, o_ref, vbuf, logit_ref, sem, *, n_seq, n_pages, n_seq_first):
    b = pl.program_id(0)
    slot = b % 2

    def score_tile(logical):
        return jnp.where(b < n_seq_first, sc0_ref[0, logical], sc1_ref[0, logical])

    def picked_page(seq, h, r, j):
        n = idx_ref[(seq * TOP_K + r) * N_HEADS + h]
        return n, n * PAGES_PER_BLOCK + j

    def copies(seq, slot_):
        out = []
        for h in range(N_HEADS):
            for r in range(TOP_K):
                for j in range(PAGES_PER_BLOCK):
                    _, logical = picked_page(seq, h, r, j)
                    page = pt_ref[seq * n_pages + logical]
                    c = (h * TOP_K + r) * PAGES_PER_BLOCK + j
                    out.append(pltpu.make_async_copy(cv_ref.at[0, page, h], vbuf.at[slot_, c], sem.at[slot_]))
        return out

    @pl.when(b == 0)
    def _():
        for c in copies(0, 0):
            c.start()

    @pl.when(b + 1 < n_seq)
    def _():
        for c in copies(b + 1, 1 - slot):
            c.start()

    last_block = n_pages // PAGES_PER_BLOCK - 1
    for h in range(N_HEADS):
        for r in range(TOP_K):
            for j in range(PAGES_PER_BLOCK):
                n, logical = picked_page(b, h, r, j)
                piece = score_tile(logical)[h:h + 1, :]
                near = brow_ref[h][:, j * PAGE:(j + 1) * PAGE]
                piece = piece + jnp.where(n == last_block, near, jnp.zeros_like(near))
                c = r * PAGES_PER_BLOCK + j
                logit_ref[h:h + 1, c * PAGE:(c + 1) * PAGE] = piece
    logits = logit_ref[...]

    q = q_ref[0]
    own = jnp.sum(q * kn_ref[0], axis=-1, keepdims=True) + ownb_ref[:, 0, 0:1]
    m = jnp.maximum(jnp.max(logits, axis=-1, keepdims=True), own)
    p = jnp.exp(logits - m)
    p_own = jnp.exp(own - m)
    denom = jnp.sum(p, axis=-1, keepdims=True) + p_own

    p_wide = jnp.concatenate([p] * N_HEADS, axis=1)
    col = lax.broadcasted_iota(jnp.int32, p_wide.shape, 1)
    row = lax.broadcasted_iota(jnp.int32, p_wide.shape, 0)
    p_wide = jnp.where(col // KEYS_PER_HEAD == row, p_wide, 0.0)
    p16 = jnp.concatenate([p_wide, jnp.zeros_like(p_wide)], axis=0).astype(BF16)

    pltpu.make_async_copy(cv_ref.at[0, pl.ds(0, N_CHUNKS), 0], vbuf.at[slot], sem.at[slot]).wait()
    vt = jnp.concatenate([vbuf[slot, c] for c in range(N_CHUNKS)], axis=1).astype(BF16)
    ctx = _dot_nt(p16, vt)[:N_HEADS] + p_own * vn_ref[0]
    o_ref[0] = ctx / denom


def _sample_attention(page_table_flat, idx_flat, score_parts, q_heads, k_new, v_new, brow, ownb, cache_vt, n_pages):
    n_seq = q_heads.shape[0]
    head_spec = pl.BlockSpec((1, N_HEADS, HEAD_DIM), lambda b, pt, ix: (b, 0, 0))
    starts = [0, score_parts[0].shape[0]]

    def part_spec(k):
        last = score_parts[k].shape[0] - 1
        return pl.BlockSpec((1, n_pages, N_HEADS, PAGE),
                            lambda b, pt, ix: (jnp.clip(b - starts[k], 0, last), 0, 0, 0))

    return pl.pallas_call(
        functools.partial(_sample_attn_kernel, n_seq=n_seq, n_pages=n_pages, n_seq_first=starts[1]),
        grid_spec=pltpu.PrefetchScalarGridSpec(
            num_scalar_prefetch=2,
            grid=(n_seq,),
            in_specs=[part_spec(0), part_spec(1),
                      head_spec, head_spec, head_spec,
                      pl.BlockSpec((N_HEADS, 1, BLOCK), lambda b, pt, ix: (0, 0, 0)),
                      pl.BlockSpec((N_HEADS, 1, 128), lambda b, pt, ix: (0, 0, 0)),
                      pl.BlockSpec(memory_space=pl.ANY)],
            out_specs=head_spec,
            scratch_shapes=[pltpu.VMEM((2, N_CHUNKS, HEAD_DIM, PAGE), F32),
                            pltpu.VMEM((N_HEADS, KEYS_PER_HEAD), F32),
                            pltpu.SemaphoreType.DMA((2,))],
        ),
        out_shape=jax.ShapeDtypeStruct((n_seq, N_HEADS, HEAD_DIM), F32),
        compiler_params=pltpu.CompilerParams(dimension_semantics=("arbitrary",),
                                             vmem_limit_bytes=VMEM_LIMIT),
    )(page_table_flat, idx_flat, *score_parts, q_heads, k_new, v_new, brow, ownb, cache_vt)


def kernel(x_prompt, x_sample, cache_k, cache_v, page_table, state_conv, state_ffn, norm1_w, w_in, q_norm_w, k_norm_w, conv_w, w_attn_up, w_conv_out, w_o, norm2_w, w_ffn_up, ffn_conv_w, w_ffn_down, rel_bias):
    batch, seq, _ = x_prompt.shape
    n_seq = x_sample.shape[0]
    n_pages = page_table.shape[1]

    w_in_b = w_in[0].astype(BF16)
    wa = w_attn_up[0].astype(BF16)
    wc = w_conv_out[0].astype(BF16)
    wo = w_o[0].astype(BF16)
    wup = w_ffn_up[0].astype(BF16)
    wdn = w_ffn_down[0].astype(BF16)
    fcw = ffn_conv_w[0]
    n1 = norm1_w[0].reshape(1, D_MODEL)
    n2 = norm2_w[0].reshape(1, D_MODEL)
    qn = jnp.tile(q_norm_w[0], N_HEADS).reshape(1, ATTN_W)
    kn = jnp.tile(k_norm_w[0], N_HEADS).reshape(1, ATTN_W)
    lane = np.arange(BLOCK)
    bd = jnp.asarray((lane[:, None] // HEAD_DIM == lane[None, :] // HEAD_DIM) / HEAD_DIM, BF16)
    cw = conv_w[0]

    bias_tiles, bias_row, bias_own = _bias_tables(rel_bias)

    xs = x_sample.reshape(n_seq, D_MODEL)
    (q_s, k_s, v_s, sin_s, ga_s, gb_s, u_s) = _inproj(
        xs, n1, w_in_b, qn, kn, bd, cw, (state_conv[0, :, 0, :], state_conv[0, :, 1, :]), sample=True)
    pt_flat = page_table.reshape(-1)
    cache_kt = cache_k.transpose(0, 1, 3, 4, 2)
    cache_vt = cache_v.transpose(0, 1, 3, 4, 2)
    xp = x_prompt.reshape(batch * seq, D_MODEL)
    n_attn_pages = n_seq * n_pages - (batch * seq // TOKEN_TILE) * FFN_STREAM_PAGES
    (q_b, kt_p, vt_p, k_b, vt_b, ksum_p, sin_p, ga_p, gb_p, conv_p) = _inproj(
        xp, n1, w_in_b, qn, kn, bd, cw, None, sample=False, seq_len=seq)

    attn_p, scores_0 = _prompt_attention(q_b.reshape(batch, seq, ATTN_W), k_b.reshape(batch, seq, ATTN_W), vt_b,
                                         ksum_p.reshape(batch, seq // BLOCK, ATTN_W), bias_tiles,
                                         pt_flat, q_s, cache_kt, n_pages, n_attn_pages)
    y_p, ffn_p, scores_1 = _merge_ffn(xp, attn_p.reshape(batch * seq, ATTN_W), sin_p, ga_p, gb_p,
                                      wa, wc, wo, n2, wup, fcw, wdn, None, sample=False, seq_len=seq,
                                      stream=(pt_flat, q_s.reshape(n_seq, 1, ATTN_W), cache_kt, n_pages, n_attn_pages))

    score_parts = [sc.reshape(-1, n_pages, N_HEADS, PAGE) for sc in (scores_0, scores_1)]
    idx = jnp.concatenate([_select_blocks(sc)[..., 0] for sc in score_parts], axis=0)
    attn_s = _sample_attention(pt_flat, idx.reshape(-1), score_parts, q_s.reshape(n_seq, N_HEADS, HEAD_DIM),
                               k_s.reshape(n_seq, N_HEADS, HEAD_DIM), v_s.reshape(n_seq, N_HEADS, HEAD_DIM),
                               bias_row, bias_own, cache_vt, n_pages)

    y_s, g_s = _merge_ffn(xs, attn_s.reshape(n_seq, ATTN_W), sin_s, ga_s, gb_s,
                          wa, wc, wo, n2, wup, fcw, wdn,
                          (state_ffn[0, :, 0, :], state_ffn[0, :, 1, :]), sample=True)

    return (
        y_p.reshape(batch, seq, D_MODEL),
        y_s.reshape(n_seq, 1, D_MODEL),
        kt_p.reshape(1, batch, N_HEADS, HEAD_DIM, seq).transpose(0, 1, 4, 2, 3),
        vt_p.reshape(1, batch, N_HEADS, HEAD_DIM, seq).transpose(0, 1, 4, 2, 3),
        conv_p[None],
        ffn_p[None],
        k_s.reshape(1, n_seq, 1, N_HEADS, HEAD_DIM),
        v_s.reshape(1, n_seq, 1, N_HEADS, HEAD_DIM),
        jnp.stack([state_conv[0, :, 1, :], u_s], axis=1)[None],
        jnp.stack([state_ffn[0, :, 1, :], g_s], axis=1)[None],
    )
```

```python
import functools
import math

import numpy as np
import jax
import jax.numpy as jnp
from jax import lax
from jax.experimental import pallas as pl
from jax.experimental.pallas import tpu as pltpu

F32 = jnp.float32
BF16 = jnp.bfloat16

D_MODEL = 1024
N_HEADS = 8
HEAD_DIM = 64
ATTN_W = N_HEADS * HEAD_DIM
CONV_CH = D_MODEL // 2
D_FF = 2816
BLOCK = 256
TOP_K = 3
PAGE = 128
NUM_BUCKETS = 32
MAX_DISTANCE = 128
EPS = 1e-6
N_COLS = 3 * ATTN_W + 3 * CONV_CH + 2 * D_MODEL

FF_CHUNK = 256
N_FF_CHUNKS = D_FF // FF_CHUNK
TOKEN_TILE = 512
NEG_BIG = -1e30
VMEM_LIMIT = 60 * 1024 * 1024
LOG2E = math.log2(math.e)


def _bucket_thresholds():
    max_exact = NUM_BUCKETS // 2
    d = np.arange(0, 4 * MAX_DISTANCE)
    logd = np.log(np.maximum(d, 1) / max_exact)
    large = max_exact + (logd / math.log(MAX_DISTANCE / max_exact) * (NUM_BUCKETS - max_exact)).astype(np.int32)
    bucket = np.where(d < max_exact, d, np.minimum(large, NUM_BUCKETS - 1))
    return [int(np.argmax(bucket >= b)) for b in range(NUM_BUCKETS)]


BUCKET_THR = _bucket_thresholds()


def _dot(a, b):
    return jnp.dot(a, b, preferred_element_type=F32)


def _dot_nt(a, b):
    return lax.dot_general(a, b, (((1,), (1,)), ((), ())), preferred_element_type=F32)


def _sigmoid(x):
    return 0.5 * jnp.tanh(0.5 * x) + 0.5


def _const_spec(shape):
    nd = len(shape)
    return pl.BlockSpec(shape, lambda *_: (0,) * nd, pipeline_mode=pl.Buffered(1))


def _bias_from_distance(d, rb_ref, h, b_far):
    val = jnp.zeros(d.shape, F32)
    for b in range(NUM_BUCKETS - 2, -1, -1):
        val = jnp.where(d < BUCKET_THR[b + 1], rb_ref[b, h] - b_far, val)
    return val


def _bias_kernel(rb_ref, tile_ref, row_ref, own_ref):
    hp = pl.program_id(0)
    j = lax.broadcasted_iota(jnp.int32, (BLOCK, BLOCK), 0)
    i = lax.broadcasted_iota(jnp.int32, (BLOCK, BLOCK), 1)
    jr = lax.broadcasted_iota(jnp.int32, (1, BLOCK), 1)
    for e in range(2):
        h = 2 * hp + e
        b_far = rb_ref[NUM_BUCKETS - 1, h]
        lanes = slice(e * BLOCK, (e + 1) * BLOCK)
        tile_ref[0, 0, :, lanes] = jnp.where(j <= i, _bias_from_distance(i - j, rb_ref, h, b_far) * LOG2E, NEG_BIG)
        tile_ref[0, 1, :, lanes] = _bias_from_distance(i - j + BLOCK, rb_ref, h, b_far) * LOG2E
        row_ref[e] = _bias_from_distance(BLOCK - jr, rb_ref, h, b_far)
        own_ref[e] = jnp.zeros((1, 128), F32) + (rb_ref[0, h] - b_far)
    tile_ref[0, 2] = jnp.zeros((BLOCK, 2 * BLOCK), F32)


def _bias_tables(rel_bias):
    n_hp = N_HEADS // 2
    return pl.pallas_call(
        _bias_kernel,
        grid=(n_hp,),
        in_specs=[pl.BlockSpec(memory_space=pltpu.SMEM)],
        out_specs=[
            pl.BlockSpec((1, 3, BLOCK, 2 * BLOCK), lambda hp: (hp, 0, 0, 0)),
            pl.BlockSpec((2, 1, BLOCK), lambda hp: (hp, 0, 0)),
            pl.BlockSpec((2, 1, 128), lambda hp: (hp, 0, 0)),
        ],
        out_shape=[
            jax.ShapeDtypeStruct((n_hp, 3, BLOCK, 2 * BLOCK), F32),
            jax.ShapeDtypeStruct((N_HEADS, 1, BLOCK), F32),
            jax.ShapeDtypeStruct((N_HEADS, 1, 128), F32),
        ],
    )(rel_bias)


def _inproj_kernel(*refs, tm, sample, tiles_per_seq):
    if sample:
        (x_ref, n1_ref, w_ref, qn_ref, kn_ref, bd_ref, cw_ref, st0_ref, st1_ref,
         q_ref, k_ref, v_ref, sin_ref, ga_ref, gb_ref, u_ref) = refs
    else:
        (x_ref, n1_ref, w_ref, qn_ref, kn_ref, bd_ref, cw_ref,
         q_ref, kt_ref, vt_ref, kb_ref, vtb_ref, ksum_ref, sin_ref, ga_ref, gb_ref, cst_ref,
         ucar_ref) = refs

    x = x_ref[...]
    ms = jnp.mean(x * x, axis=-1, keepdims=True)
    xn = (x * lax.rsqrt(ms + EPS) * n1_ref[...]).astype(BF16)

    def proj(a, b):
        return _dot(xn, w_ref[:, a:b])

    def head_norm(t, w_row):
        sq = (t * t).astype(BF16)
        bd = bd_ref[...]
        half = ATTN_W // 2
        msq = jnp.concatenate([_dot(sq[:, :half], bd), _dot(sq[:, half:], bd)], axis=1)
        return t * lax.rsqrt(msq + EPS) * w_row

    q = head_norm(proj(0, ATTN_W), qn_ref[...]) * (HEAD_DIM ** -0.5)
    k = head_norm(proj(ATTN_W, 2 * ATTN_W), kn_ref[...])
    v = proj(2 * ATTN_W, 3 * ATTN_W)
    c0 = 3 * ATTN_W
    cb = proj(c0, c0 + CONV_CH)
    u = proj(c0 + CONV_CH, c0 + 2 * CONV_CH) * proj(c0 + 2 * CONV_CH, c0 + 3 * CONV_CH)
    g0 = c0 + 3 * CONV_CH
    ga_ref[...] = proj(g0, g0 + D_MODEL).astype(ga_ref.dtype)
    gb_ref[...] = proj(g0 + D_MODEL, g0 + 2 * D_MODEL).astype(gb_ref.dtype)

    w0, w1, w2 = cw_ref[0:1, :], cw_ref[1:2, :], cw_ref[2:3, :]
    if sample:
        q_ref[...] = q
        k_ref[...] = k
        v_ref[...] = v
        uc = st0_ref[...] * w0 + st1_ref[...] * w1 + u * w2
        u_ref[...] = u
    else:
        q_ref[...] = (q * LOG2E).astype(BF16)
        kb_ref[...] = k.astype(BF16)
        kt_ref[0] = k.T
        v_t = v.T
        vt_ref[0] = v_t
        for r in range(tm // BLOCK):
            vtb_ref[0, r] = v_t[:, r * BLOCK:(r + 1) * BLOCK].astype(BF16)
            ksum_ref[0, r:r + 1, :] = jnp.sum(k[r * BLOCK:(r + 1) * BLOCK], axis=0, keepdims=True)

        @pl.when(pl.program_id(0) % tiles_per_seq == 0)
        def _():
            ucar_ref[...] = jnp.zeros_like(ucar_ref)

        ue = jnp.concatenate([ucar_ref[...], u], axis=0)
        uc = ue[6:6 + tm] * w0 + ue[7:7 + tm] * w1 + u * w2
        ucar_ref[...] = u[tm - 8:tm]
        cst_ref[0] = u[tm - 2:tm]
    sin_ref[...] = (cb * uc).astype(sin_ref.dtype)


def _inproj(x, n1, w_in, qn, kn, bd, conv_w, states, *, sample, seq_len=1):
    n_tok = x.shape[0]
    tm = n_tok if sample else TOKEN_TILE
    n_tiles = n_tok // tm
    tiles_per_seq = 1 if sample else seq_len // tm

    def row(width):
        return pl.BlockSpec((tm, width), lambda i, *_: (i, 0))

    in_specs = [row(D_MODEL), _const_spec((1, D_MODEL)), _const_spec((D_MODEL, N_COLS)),
                _const_spec((1, ATTN_W)), _const_spec((1, ATTN_W)), _const_spec((BLOCK, BLOCK)),
                _const_spec((3, CONV_CH))]
    args = [x, n1, w_in, qn, kn, bd, conv_w]
    if sample:
        in_specs += [row(CONV_CH), row(CONV_CH)]
        args += list(states)
        out_specs = [row(ATTN_W), row(ATTN_W), row(ATTN_W), row(CONV_CH), row(D_MODEL), row(D_MODEL),
                     row(CONV_CH)]
        out_shape = [jax.ShapeDtypeStruct((n_tok, ATTN_W), F32)] * 3 + [
            jax.ShapeDtypeStruct((n_tok, CONV_CH), BF16),
            jax.ShapeDtypeStruct((n_tok, D_MODEL), BF16),
            jax.ShapeDtypeStruct((n_tok, D_MODEL), BF16),
            jax.ShapeDtypeStruct((n_tok, CONV_CH), F32)]
        scratch = []
    else:
        blocks_per_tile = tm // BLOCK
        n_seq = n_tiles // tiles_per_seq
        t_spec = pl.BlockSpec((1, ATTN_W, tm), lambda i, *_: (i // tiles_per_seq, 0, i % tiles_per_seq))
        out_specs = [row(ATTN_W), t_spec, t_spec, row(ATTN_W),
                     pl.BlockSpec((1, blocks_per_tile, ATTN_W, BLOCK),
                                  lambda i, *_: (i // tiles_per_seq, i % tiles_per_seq, 0, 0)),
                     pl.BlockSpec((1, blocks_per_tile, ATTN_W), lambda i, *_: (i, 0, 0)),
                     row(CONV_CH), row(D_MODEL), row(D_MODEL),
                     pl.BlockSpec((1, 2, CONV_CH), lambda i, *_: (i // tiles_per_seq, 0, 0))]
        out_shape = [jax.ShapeDtypeStruct((n_tok, ATTN_W), BF16),
                     jax.ShapeDtypeStruct((n_seq, ATTN_W, seq_len), F32),
                     jax.ShapeDtypeStruct((n_seq, ATTN_W, seq_len), F32),
                     jax.ShapeDtypeStruct((n_tok, ATTN_W), BF16),
                     jax.ShapeDtypeStruct((n_seq, seq_len // BLOCK, ATTN_W, BLOCK), BF16),
                     jax.ShapeDtypeStruct((n_tiles, blocks_per_tile, ATTN_W), F32),
                     jax.ShapeDtypeStruct((n_tok, CONV_CH), BF16),
                     jax.ShapeDtypeStruct((n_tok, D_MODEL), BF16),
                     jax.ShapeDtypeStruct((n_tok, D_MODEL), BF16),
                     jax.ShapeDtypeStruct((n_tiles // tiles_per_seq, 2, CONV_CH), F32)]
        scratch = [pltpu.VMEM((8, CONV_CH), F32)]
    return pl.pallas_call(
        functools.partial(_inproj_kernel, tm=tm, sample=sample, tiles_per_seq=tiles_per_seq),
        grid=(n_tiles,),
        in_specs=in_specs,
        out_specs=out_specs,
        scratch_shapes=scratch,
        out_shape=out_shape,
        compiler_params=pltpu.CompilerParams(dimension_semantics=("arbitrary",),
                                             vmem_limit_bytes=VMEM_LIMIT),
    )(*args)


STREAM_SLOTS = 5


def _query_block_order(t, n_blocks):
    return jnp.where(t % 2 == 0, t // 2, n_blocks - 1 - t // 2)
FFN_STREAM_PAGES = 32


def _block_diag_query(q_row):
    row = lax.broadcasted_iota(jnp.int32, (N_HEADS, ATTN_W), 0)
    lane_head = lax.broadcasted_iota(jnp.int32, (N_HEADS, ATTN_W), 1) // HEAD_DIM
    q_rows = jnp.where(row == lane_head, q_row, 0.0)
    q_hi = q_rows.astype(BF16).astype(F32)
    return jnp.concatenate([q_hi, q_rows - q_hi], axis=0).astype(BF16)


def _page_scores(q2, page):
    s2 = _dot(q2, page.reshape(ATTN_W, PAGE).astype(BF16))
    return s2[:N_HEADS] + s2[N_HEADS:]


def _attn_kernel(pt_ref, q_ref, k_ref, vt_ref, ksum_ref, bias_ref, qs_ref, ck_ref, o_ref, sc_ref,
                 pen_ref, m_ref, l_ref, alpha_ref, acc_ref, s_ref, p_ref, pbuf, psem, *, pages_per_step, n_pages):
    n_blocks = k_ref.shape[1] // BLOCK
    qb = _query_block_order(pl.program_id(2), n_blocks)
    nq = 2 * BLOCK
    step = (pl.program_id(0) * pl.num_programs(1) + pl.program_id(1)) * pl.num_programs(2) + pl.program_id(2)
    n_steps = pl.num_programs(0) * pl.num_programs(1) * pl.num_programs(2)

    def page_copies(step_, slot_):
        return [pltpu.make_async_copy(ck_ref.at[0, pt_ref[step_ * pages_per_step + i]], pbuf.at[slot_, i],
                                      psem.at[slot_]) for i in range(pages_per_step)]

    @pl.when(step == 0)
    def _():
        for ahead in range(STREAM_SLOTS - 1):
            for c in page_copies(ahead, ahead):
                c.start()

    @pl.when(step + (STREAM_SLOTS - 1) < n_steps)
    def _():
        nxt = step + (STREAM_SLOTS - 1)
        for c in page_copies(nxt, nxt % STREAM_SLOTS):
            c.start()

    slot = step % STREAM_SLOTS
    pltpu.make_async_copy(ck_ref.at[0, pl.ds(0, pages_per_step)], pbuf.at[slot], psem.at[slot]).wait()
    page0 = step * pages_per_step
    seq0 = page0 // n_pages
    seq1 = jnp.minimum(seq0 + 1, qs_ref.shape[0] - 1)
    q2s0 = _block_diag_query(qs_ref[pl.ds(seq0, 1), :])
    q2s1 = _block_diag_query(qs_ref[pl.ds(seq1, 1), :])

    def score_page(i):
        q2s = jnp.where((page0 + i) // n_pages == seq0, q2s0, q2s1)
        sc_ref[i] = _page_scores(q2s, pbuf[slot, i])

    pages_at_head = (pages_per_step * 5) // 8
    for i in range(pages_at_head):
        score_page(i)

    q = q_ref[0]
    lane_head = lax.broadcasted_iota(jnp.int32, q.shape, 1) // HEAD_DIM
    zero = jnp.zeros_like(q)
    q2 = jnp.concatenate([jnp.where(lane_head == 0, q, zero), jnp.where(lane_head == 1, q, zero)], axis=0)

    kmean = ksum_ref[0] * (1.0 / BLOCK)
    km_hi = kmean.astype(BF16)
    km_lo = (kmean - km_hi.astype(F32)).astype(BF16)
    g2 = _dot_nt(jnp.concatenate([km_hi, km_lo], axis=0), q2)
    g = g2[:n_blocks] + g2[n_blocks:]
    blk = lax.broadcasted_iota(jnp.int32, (n_blocks, nq), 0)
    valid = blk < qb
    g = jnp.where(valid, g, -jnp.inf)
    rank = jnp.zeros(g.shape, jnp.int32)
    for j in range(n_blocks):
        gj = g[j:j + 1, :]
        beats = (gj > g) | ((gj == g) & (blk > j))
        rank = rank + beats.astype(jnp.int32)
    sel = valid & (rank < TOP_K)
    pen_ref[0:n_blocks] = jnp.where(sel, 0.0, NEG_BIG)
    pen_ref[n_blocks:2 * n_blocks] = jnp.zeros((n_blocks, nq), F32)

    def item(j):
        own = j == 0
        kb = jnp.where(own, qb, j - 1)
        tile = jnp.where(own, 0, jnp.where(j == qb, 1, 2))
        pen_row = jnp.where(own, n_blocks, j - 1)
        return kb, tile, pen_row

    def item_scores(j):
        kb, tile, pen_row = item(j)
        kblk = k_ref[0, pl.ds(pl.multiple_of(kb * BLOCK, BLOCK), BLOCK), :]
        return _dot_nt(kblk, q2) + pen_ref[pl.ds(pen_row, 1), :] + bias_ref[0, tile]

    def pv(kb, p):
        vt = vt_ref[0, kb]
        return jnp.concatenate([_dot(vt[:HEAD_DIM], p[:, :BLOCK]), _dot(vt[HEAD_DIM:], p[:, BLOCK:])], axis=1)

    n_items = qb + 1
    s_ref[...] = item_scores(0)
    m_ref[...] = jnp.full(m_ref.shape, NEG_BIG, F32)
    l_ref[...] = jnp.zeros(l_ref.shape, F32)
    alpha_ref[...] = jnp.ones(alpha_ref.shape, F32)
    acc_ref[...] = jnp.zeros(acc_ref.shape, F32)
    p_ref[...] = jnp.zeros(p_ref.shape, BF16)

    def trip(j, carry):
        s = s_ref[...]
        s_next = item_scores(jnp.minimum(j + 1, n_items - 1))
        kb_prev, _, _ = item(jnp.maximum(j - 1, 0))
        acc = alpha_ref[...] * acc_ref[...] + pv(kb_prev, p_ref[...])
        m_old = m_ref[...]
        m_new = jnp.maximum(m_old, jnp.max(s, axis=0, keepdims=True))
        alpha = jnp.exp2(m_old - m_new)
        p = jnp.exp2(s - m_new)
        l_ref[...] = alpha * l_ref[...] + jnp.sum(p, axis=0, keepdims=True)
        m_ref[...] = m_new
        acc_ref[...] = acc
        alpha_ref[...] = alpha
        p_ref[...] = p.astype(BF16)
        s_ref[...] = s_next
        return carry

    lax.fori_loop(0, n_items, trip, 0)
    kb_last, _, _ = item(qb)
    acc = alpha_ref[...] * acc_ref[...] + pv(kb_last, p_ref[...])
    out = acc / l_ref[...]
    out_t = jnp.concatenate([out[:, :BLOCK], out[:, BLOCK:]], axis=0)
    o_ref[0] = out_t.T.astype(o_ref.dtype)

    for i in range(pages_at_head, pages_per_step):
        score_page(i)


def _prompt_attention(q, k, vt_blocks, ksum, bias_tiles, page_table_flat, q_sample, cache_kt, n_pages,
                      n_stream_pages):
    b, s, _ = q.shape
    n_blocks = s // BLOCK
    n_hp = N_HEADS // 2
    n_steps = b * n_hp * n_blocks
    n_seq = q_sample.shape[0]
    pages_per_step = n_stream_pages // n_steps
    assert pages_per_step * n_steps == n_stream_pages and pages_per_step <= n_pages

    def step_of(bi, hp, qb):
        return (bi * n_hp + hp) * n_blocks + qb

    attn, scores = pl.pallas_call(
        functools.partial(_attn_kernel, pages_per_step=pages_per_step, n_pages=n_pages),
        grid_spec=pltpu.PrefetchScalarGridSpec(
            num_scalar_prefetch=1,
            grid=(b, n_hp, n_blocks),
            in_specs=[
                pl.BlockSpec((1, BLOCK, 128), lambda bi, hp, qb, pt: (bi, _query_block_order(qb, n_blocks), hp)),
                pl.BlockSpec((1, s, 128), lambda bi, hp, qb, pt: (bi, 0, hp)),
                pl.BlockSpec((1, n_blocks, 128, BLOCK), lambda bi, hp, qb, pt: (bi, 0, hp, 0)),
                pl.BlockSpec((1, n_blocks, 128), lambda bi, hp, qb, pt: (bi, 0, hp)),
                pl.BlockSpec((1, 3, BLOCK, 2 * BLOCK), lambda bi, hp, qb, pt: (hp, 0, 0, 0)),
                pl.BlockSpec((n_seq, ATTN_W), lambda bi, hp, qb, pt: (0, 0), pipeline_mode=pl.Buffered(1)),
                pl.BlockSpec(memory_space=pl.ANY),
            ],
            out_specs=[
                pl.BlockSpec((1, BLOCK, 128), lambda bi, hp, qb, pt: (bi, _query_block_order(qb, n_blocks), hp)),
                pl.BlockSpec((pages_per_step, N_HEADS, PAGE), lambda bi, hp, qb, pt: (step_of(bi, hp, qb), 0, 0)),
            ],
            scratch_shapes=[
                pltpu.VMEM((2 * n_blocks, 2 * BLOCK), F32),
                pltpu.VMEM((1, 2 * BLOCK), F32),
                pltpu.VMEM((1, 2 * BLOCK), F32),
                pltpu.VMEM((1, 2 * BLOCK), F32),
                pltpu.VMEM((HEAD_DIM, 2 * BLOCK), F32),
                pltpu.VMEM((BLOCK, 2 * BLOCK), F32),
                pltpu.VMEM((BLOCK, 2 * BLOCK), BF16),
                pltpu.VMEM((STREAM_SLOTS, pages_per_step, N_HEADS, HEAD_DIM, PAGE), F32),
                pltpu.SemaphoreType.DMA((STREAM_SLOTS,)),
            ],
        ),
        out_shape=[jax.ShapeDtypeStruct((b, s, ATTN_W), BF16),
                   jax.ShapeDtypeStruct((n_stream_pages, N_HEADS, PAGE), F32)],
        compiler_params=pltpu.CompilerParams(
            dimension_semantics=("arbitrary", "arbitrary", "arbitrary"),
            vmem_limit_bytes=VMEM_LIMIT),
    )(page_table_flat, q, k, vt_blocks, ksum, bias_tiles, q_sample, cache_kt)
    return attn, scores


def _merge_ffn_kernel(*refs, tm, sample, tiles_per_seq, stream):
    if sample:
        (x_ref, at_ref, sin_ref, ga_ref, gb_ref, wa_ref, wc_ref, wo_ref, n2_ref, wup_ref, fcw_ref,
         wdn_ref, fs0_ref, fs1_ref, y_ref, g_ref, acc_ref) = refs
    else:
        (pt_ref, x_ref, at_ref, sin_ref, ga_ref, gb_ref, wa_ref, wc_ref, wo_ref, n2_ref, wup_ref, fcw_ref,
         wdn_ref, qs_ref, ck_ref, y_ref, fst_ref, sc_ref, acc_ref, gcar_ref, pbuf, psem) = refs
        first_page, pages_per_step = stream
        page0 = first_page + pl.program_id(0) * pages_per_step
        for i in range(pages_per_step):
            pltpu.make_async_copy(ck_ref.at[0, pt_ref[page0 + i]], pbuf.at[i], psem.at[0]).start()

    a = _dot(at_ref[...].astype(BF16), wa_ref[...])
    b = _dot(sin_ref[...], wc_ref[...])
    merged = _sigmoid(ga_ref[...].astype(F32)) * a + _sigmoid(gb_ref[...].astype(F32)) * b
    x1 = x_ref[...] + _dot(merged.astype(BF16), wo_ref[...])
    ms = jnp.mean(x1 * x1, axis=-1, keepdims=True)
    xn2 = (x1 * lax.rsqrt(ms + EPS) * n2_ref[...]).astype(BF16)
    acc_ref[...] = x1

    if not sample:
        @pl.when(pl.program_id(0) % tiles_per_seq == 0)
        def _():
            gcar_ref[...] = jnp.zeros_like(gcar_ref)

    for c in range(N_FF_CHUNKS):
        cols = slice(c * FF_CHUNK, (c + 1) * FF_CHUNK)
        g = _dot(xn2, wup_ref[:, cols])
        u = _dot(xn2, wup_ref[:, D_FF + c * FF_CHUNK:D_FF + (c + 1) * FF_CHUNK])
        w0, w1, w2 = fcw_ref[0:1, cols], fcw_ref[1:2, cols], fcw_ref[2:3, cols]
        if sample:
            gc = fs0_ref[:, cols] * w0 + fs1_ref[:, cols] * w1 + g * w2
            g_ref[:, cols] = g
        else:
            ge = jnp.concatenate([gcar_ref[:, cols], g], axis=0)
            gc = ge[6:6 + tm] * w0 + ge[7:7 + tm] * w1 + g * w2
            gcar_ref[:, cols] = g[tm - 8:tm]
            fst_ref[0, :, cols] = g[tm - 2:tm]
        hidden = (gc * _sigmoid(gc) * u).astype(BF16)
        acc_ref[...] += _dot(hidden, wdn_ref[cols, :])

        if not sample and c == N_FF_CHUNKS // 2:
            pltpu.make_async_copy(ck_ref.at[0, pl.ds(0, pages_per_step)], pbuf, psem.at[0]).wait()
            q2s = _block_diag_query(qs_ref[0])
            for i in range(pages_per_step):
                sc_ref[i] = _page_scores(q2s, pbuf[i])
    y_ref[...] = acc_ref[...]


def _merge_ffn(x, attn, s_in, ga, gb, wa, wc, wo, n2, wup, fcw, wdn, states, *, sample, seq_len=1, stream=None):
    n_tok = x.shape[0]
    tm = n_tok if sample else TOKEN_TILE
    n_tiles = n_tok // tm
    tiles_per_seq = 1 if sample else seq_len // tm

    def row(width):
        return pl.BlockSpec((tm, width), lambda i, *_: (i, 0))

    in_specs = [row(D_MODEL), row(ATTN_W), row(CONV_CH), row(D_MODEL), row(D_MODEL),
                _const_spec((ATTN_W, D_MODEL)), _const_spec((CONV_CH, D_MODEL)),
                _const_spec((D_MODEL, D_MODEL)), _const_spec((1, D_MODEL)),
                _const_spec((D_MODEL, 2 * D_FF)), _const_spec((3, D_FF)), _const_spec((D_FF, D_MODEL))]
    args = [x, attn, s_in, ga, gb, wa, wc, wo, n2, wup, fcw, wdn]
    scratch = [pltpu.VMEM((tm, D_MODEL), F32)]
    if sample:
        in_specs += [row(D_FF), row(D_FF)]
        args += list(states)
        out_specs = [row(D_MODEL), row(D_FF)]
        out_shape = [jax.ShapeDtypeStruct((n_tok, D_MODEL), F32),
                     jax.ShapeDtypeStruct((n_tok, D_FF), F32)]
    else:
        n_seq = n_tiles // tiles_per_seq
        page_table_flat, q_sample, cache_kt, n_pages, first_page = stream
        pps = FFN_STREAM_PAGES
        assert first_page % n_pages == 0 and n_pages % pps == 0
        in_specs += [pl.BlockSpec((1, 1, ATTN_W), lambda i, *_: ((first_page + i * pps) // n_pages, 0, 0)),
                     pl.BlockSpec(memory_space=pl.ANY)]
        args = [page_table_flat] + args + [q_sample, cache_kt]
        out_specs = [row(D_MODEL), pl.BlockSpec((1, 2, D_FF), lambda i, *_: (i // tiles_per_seq, 0, 0)),
                     pl.BlockSpec((pps, N_HEADS, PAGE), lambda i, *_: (i, 0, 0))]
        out_shape = [jax.ShapeDtypeStruct((n_tok, D_MODEL), F32),
                     jax.ShapeDtypeStruct((n_seq, 2, D_FF), F32),
                     jax.ShapeDtypeStruct((n_tiles * pps, N_HEADS, PAGE), F32)]
        scratch += [pltpu.VMEM((8, D_FF), F32),
                    pltpu.VMEM((pps, N_HEADS, HEAD_DIM, PAGE), F32),
                    pltpu.SemaphoreType.DMA((1,))]
    return pl.pallas_call(
        functools.partial(_merge_ffn_kernel, tm=tm, sample=sample, tiles_per_seq=tiles_per_seq,
                          stream=None if sample else (stream[4], FFN_STREAM_PAGES)),
        grid_spec=pltpu.PrefetchScalarGridSpec(
            num_scalar_prefetch=0 if sample else 1,
            grid=(n_tiles,),
            in_specs=in_specs,
            out_specs=out_specs,
            scratch_shapes=scratch,
        ),
        out_shape=out_shape,
        compiler_params=pltpu.CompilerParams(dimension_semantics=("arbitrary",),
                                             vmem_limit_bytes=VMEM_LIMIT),
    )(*args)


SEQ_PER_SELECT_STEP = 8


def _select_kernel(sc_ref, idx_ref):
    n_pages = sc_ref.shape[1]
    pages_per_block = BLOCK // PAGE
    n_blocks = n_pages // pages_per_block
    blk = lax.broadcasted_iota(jnp.int32, (n_blocks, N_HEADS, 128), 0)
    for s in range(SEQ_PER_SELECT_STEP):
        page_sum = jnp.sum(sc_ref[s], axis=-1, keepdims=True)
        block_sum = jnp.sum(page_sum.reshape(n_blocks, pages_per_block, N_HEADS, 1), axis=1)
        gate = jnp.broadcast_to(block_sum * (1.0 / BLOCK), (n_blocks, N_HEADS, 128))
        for r in range(TOP_K):
            best = jnp.max(gate, axis=0, keepdims=True)
            idx = jnp.min(jnp.where(gate == best, blk, n_blocks), axis=0, keepdims=True)
            idx_ref[s, r] = idx[0]
            gate = jnp.where(blk == idx, -jnp.inf, gate)


def _select_blocks(scores):
    n_seq, n_pages = scores.shape[:2]
    assert n_seq % SEQ_PER_SELECT_STEP == 0
    return pl.pallas_call(
        _select_kernel,
        grid=(n_seq // SEQ_PER_SELECT_STEP,),
        in_specs=[pl.BlockSpec((SEQ_PER_SELECT_STEP, n_pages, N_HEADS, PAGE), lambda i: (i, 0, 0, 0))],
        out_specs=pl.BlockSpec((SEQ_PER_SELECT_STEP, TOP_K, N_HEADS, 128), lambda i: (i, 0, 0, 0)),
        out_shape=jax.ShapeDtypeStruct((n_seq, TOP_K, N_HEADS, 128), jnp.int32),
    )(scores)


PAGES_PER_BLOCK = BLOCK // PAGE
CHUNKS_PER_HEAD = TOP_K * PAGES_PER_BLOCK
N_CHUNKS = N_HEADS * CHUNKS_PER_HEAD
KEYS_PER_HEAD = CHUNKS_PER_HEAD * PAGE


SEQ_PER_ATTN_STEP = 4


def _sample_attn_kernel(pt_ref, idx_ref, sc0_ref, sc1_ref, q_ref, kn_ref, vn_ref, brow_ref, ownb_ref,
                        cv_ref, o_ref, vbuf, logit_ref, sem, *, n_steps, n_pages, n_seq_first):
    g = pl.program_id(0)
    slot = g % 2

    def picked_page(seq, h, r, j):
        n = idx_ref[(seq * TOP_K + r) * N_HEADS + h]
        return n, n * PAGES_PER_BLOCK + j

    def copies(step_, slot_):
        out = []
        for s in range(SEQ_PER_ATTN_STEP):
            seq = step_ * SEQ_PER_ATTN_STEP + s
            for h in range(N_HEADS):
                for r in range(TOP_K):
                    for j in range(PAGES_PER_BLOCK):
                        _, logical = picked_page(seq, h, r, j)
                        page = pt_ref[seq * n_pages + logical]
                        c = s * N_CHUNKS + (h * TOP_K + r) * PAGES_PER_BLOCK + j
                        out.append(pltpu.make_async_copy(cv_ref.at[0, page, h], vbuf.at[slot_, c], sem.at[slot_]))
        return out

    @pl.when(g == 0)
    def _():
        for c in copies(0, 0):
            c.start()

    @pl.when(g + 1 < n_steps)
    def _():
        for c in copies(g + 1, 1 - slot):
            c.start()

    last_block = n_pages // PAGES_PER_BLOCK - 1
    col = lax.broadcasted_iota(jnp.int32, (N_HEADS, N_HEADS * KEYS_PER_HEAD), 1)
    row = lax.broadcasted_iota(jnp.int32, (N_HEADS, N_HEADS * KEYS_PER_HEAD), 0)
    own_keys = col // KEYS_PER_HEAD == row
    per_seq = []
    for s in range(SEQ_PER_ATTN_STEP):
        seq = g * SEQ_PER_ATTN_STEP + s
        for h in range(N_HEADS):
            for r in range(TOP_K):
                for j in range(PAGES_PER_BLOCK):
                    n, logical = picked_page(seq, h, r, j)
                    tile = jnp.where(seq < n_seq_first, sc0_ref[s, logical], sc1_ref[s, logical])
                    near = brow_ref[h][:, j * PAGE:(j + 1) * PAGE]
                    piece = tile[h:h + 1, :] + jnp.where(n == last_block, near, jnp.zeros_like(near))
                    c = r * PAGES_PER_BLOCK + j
                    logit_ref[s, h:h + 1, c * PAGE:(c + 1) * PAGE] = piece
        logits = logit_ref[s]

        q = q_ref[s]
        own = jnp.sum(q * kn_ref[s], axis=-1, keepdims=True) + ownb_ref[:, 0, 0:1]
        m = jnp.maximum(jnp.max(logits, axis=-1, keepdims=True), own)
        p = jnp.exp(logits - m)
        p_own = jnp.exp(own - m)
        denom = jnp.sum(p, axis=-1, keepdims=True) + p_own

        p_wide = jnp.where(own_keys, jnp.concatenate([p] * N_HEADS, axis=1), 0.0)
        p16 = jnp.concatenate([p_wide, jnp.zeros_like(p_wide)], axis=0).astype(BF16)
        per_seq.append((p16, p_own, denom))

    pltpu.make_async_copy(cv_ref.at[0, pl.ds(0, SEQ_PER_ATTN_STEP * N_CHUNKS), 0], vbuf.at[slot],
                          sem.at[slot]).wait()
    for s, (p16, p_own, denom) in enumerate(per_seq):
        vt = jnp.concatenate([vbuf[slot, s * N_CHUNKS + c] for c in range(N_CHUNKS)], axis=1).astype(BF16)
        ctx = _dot_nt(p16, vt)[:N_HEADS] + p_own * vn_ref[s]
        o_ref[s] = ctx / denom


def _sample_attention(page_table_flat, idx_flat, score_parts, q_heads, k_new, v_new, brow, ownb, cache_vt, n_pages):
    n_seq = q_heads.shape[0]
    sps = SEQ_PER_ATTN_STEP
    n_first = score_parts[0].shape[0]
    assert n_seq % sps == 0 and n_first % sps == 0
    n_steps = n_seq // sps
    head_spec = pl.BlockSpec((sps, N_HEADS, HEAD_DIM), lambda g, pt, ix: (g, 0, 0))
    starts = [0, n_first // sps]

    def part_spec(k):
        last = score_parts[k].shape[0] // sps - 1
        return pl.BlockSpec((sps, n_pages, N_HEADS, PAGE),
                            lambda g, pt, ix: (jnp.clip(g - starts[k], 0, last), 0, 0, 0))

    return pl.pallas_call(
        functools.partial(_sample_attn_kernel, n_steps=n_steps, n_pages=n_pages, n_seq_first=n_first),
        grid_spec=pltpu.PrefetchScalarGridSpec(
            num_scalar_prefetch=2,
            grid=(n_steps,),
            in_specs=[part_spec(0), part_spec(1),
                      head_spec, head_spec, head_spec,
                      pl.BlockSpec((N_HEADS, 1, BLOCK), lambda g, pt, ix: (0, 0, 0)),
                      pl.BlockSpec((N_HEADS, 1, 128), lambda g, pt, ix: (0, 0, 0)),
                      pl.BlockSpec(memory_space=pl.ANY)],
            out_specs=head_spec,
            scratch_shapes=[pltpu.VMEM((2, sps * N_CHUNKS, HEAD_DIM, PAGE), F32),
                            pltpu.VMEM((sps, N_HEADS, KEYS_PER_HEAD), F32),
                            pltpu.SemaphoreType.DMA((2,))],
        ),
        out_shape=jax.ShapeDtypeStruct((n_seq, N_HEADS, HEAD_DIM), F32),
        compiler_params=pltpu.CompilerParams(dimension_semantics=("arbitrary",),
                                             vmem_limit_bytes=VMEM_LIMIT),
    )(page_table_flat, idx_flat, *score_parts, q_heads, k_new, v_new, brow, ownb, cache_vt)


def kernel(x_prompt, x_sample, cache_k, cache_v, page_table, state_conv, state_ffn, norm1_w, w_in, q_norm_w, k_norm_w, conv_w, w_attn_up, w_conv_out, w_o, norm2_w, w_ffn_up, ffn_conv_w, w_ffn_down, rel_bias):
    batch, seq, _ = x_prompt.shape
    n_seq = x_sample.shape[0]
    n_pages = page_table.shape[1]

    w_in_b = w_in[0].astype(BF16)
    wa = w_attn_up[0].astype(BF16)
    wc = w_conv_out[0].astype(BF16)
    wo = w_o[0].astype(BF16)
    wup = w_ffn_up[0].astype(BF16)
    wdn = w_ffn_down[0].astype(BF16)
    fcw = ffn_conv_w[0]
    n1 = norm1_w[0].reshape(1, D_MODEL)
    n2 = norm2_w[0].reshape(1, D_MODEL)
    qn = jnp.tile(q_norm_w[0], N_HEADS).reshape(1, ATTN_W)
    kn = jnp.tile(k_norm_w[0], N_HEADS).reshape(1, ATTN_W)
    lane = np.arange(BLOCK)
    bd = jnp.asarray((lane[:, None] // HEAD_DIM == lane[None, :] // HEAD_DIM) / HEAD_DIM, BF16)
    cw = conv_w[0]

    bias_tiles, bias_row, bias_own = _bias_tables(rel_bias)

    xs = x_sample.reshape(n_seq, D_MODEL)
    (q_s, k_s, v_s, sin_s, ga_s, gb_s, u_s) = _inproj(
        xs, n1, w_in_b, qn, kn, bd, cw, (state_conv[0, :, 0, :], state_conv[0, :, 1, :]), sample=True)
    pt_flat = page_table.reshape(-1)
    cache_kt = cache_k.transpose(0, 1, 3, 4, 2)
    cache_vt = cache_v.transpose(0, 1, 3, 4, 2)
    xp = x_prompt.reshape(batch * seq, D_MODEL)
    n_attn_pages = n_seq * n_pages - (batch * seq // TOKEN_TILE) * FFN_STREAM_PAGES
    (q_b, kt_p, vt_p, k_b, vt_b, ksum_p, sin_p, ga_p, gb_p, conv_p) = _inproj(
        xp, n1, w_in_b, qn, kn, bd, cw, None, sample=False, seq_len=seq)

    attn_p, scores_0 = _prompt_attention(q_b.reshape(batch, seq, ATTN_W), k_b.reshape(batch, seq, ATTN_W), vt_b,
                                         ksum_p.reshape(batch, seq // BLOCK, ATTN_W), bias_tiles,
                                         pt_flat, q_s, cache_kt, n_pages, n_attn_pages)
    y_p, ffn_p, scores_1 = _merge_ffn(xp, attn_p.reshape(batch * seq, ATTN_W), sin_p, ga_p, gb_p,
                                      wa, wc, wo, n2, wup, fcw, wdn, None, sample=False, seq_len=seq,
                                      stream=(pt_flat, q_s.reshape(n_seq, 1, ATTN_W), cache_kt, n_pages, n_attn_pages))

    score_parts = [sc.reshape(-1, n_pages, N_HEADS, PAGE) for sc in (scores_0, scores_1)]
    idx = jnp.concatenate([_select_blocks(sc)[..., 0] for sc in score_parts], axis=0)
    attn_s = _sample_attention(pt_flat, idx.reshape(-1), score_parts, q_s.reshape(n_seq, N_HEADS, HEAD_DIM),
                               k_s.reshape(n_seq, N_HEADS, HEAD_DIM), v_s.reshape(n_seq, N_HEADS, HEAD_DIM),
                               bias_row, bias_own, cache_vt, n_pages)

    y_s, g_s = _merge_ffn(xs, attn_s.reshape(n_seq, ATTN_W), sin_s, ga_s, gb_s,
                          wa, wc, wo, n2, wup, fcw, wdn,
                          (state_ffn[0, :, 0, :], state_ffn[0, :, 1, :]), sample=True)

    return (
        y_p.reshape(batch, seq, D_MODEL),
        y_s.reshape(n_seq, 1, D_MODEL),
        kt_p.reshape(1, batch, N_HEADS, HEAD_DIM, seq).transpose(0, 1, 4, 2, 3),
        vt_p.reshape(1, batch, N_HEADS, HEAD_DIM, seq).transpose(0, 1, 4, 2, 3),
        conv_p[None],
        ffn_p[None],
        k_s.reshape(1, n_seq, 1, N_HEADS, HEAD_DIM),
        v_s.reshape(1, n_seq, 1, N_HEADS, HEAD_DIM),
        jnp.stack([state_conv[0, :, 1, :], u_s], axis=1)[None],
        jnp.stack([state_ffn[0, :, 1, :], g_s], axis=1)[None],
    )
```

```python
import functools
import math

import numpy as np
import jax
import jax.numpy as jnp
from jax import lax
from jax.experimental import pallas as pl
from jax.experimental.pallas import tpu as pltpu

F32 = jnp.float32
BF16 = jnp.bfloat16

D_MODEL = 1024
N_HEADS = 8
HEAD_DIM = 64
ATTN_W = N_HEADS * HEAD_DIM
CONV_CH = D_MODEL // 2
D_FF = 2816
BLOCK = 256
TOP_K = 3
PAGE = 128
NUM_BUCKETS = 32
MAX_DISTANCE = 128
EPS = 1e-6
N_COLS = 3 * ATTN_W + 3 * CONV_CH + 2 * D_MODEL

FF_CHUNK = 256
N_FF_CHUNKS = D_FF // FF_CHUNK
TOKEN_TILE = 512
NEG_BIG = -1e30
VMEM_LIMIT = 60 * 1024 * 1024
LOG2E = math.log2(math.e)


def _bucket_thresholds():
    max_exact = NUM_BUCKETS // 2
    d = np.arange(0, 4 * MAX_DISTANCE)
    logd = np.log(np.maximum(d, 1) / max_exact)
    large = max_exact + (logd / math.log(MAX_DISTANCE / max_exact) * (NUM_BUCKETS - max_exact)).astype(np.int32)
    bucket = np.where(d < max_exact, d, np.minimum(large, NUM_BUCKETS - 1))
    return [int(np.argmax(bucket >= b)) for b in range(NUM_BUCKETS)]


BUCKET_THR = _bucket_thresholds()


def _dot(a, b):
    return jnp.dot(a, b, preferred_element_type=F32)


def _dot_nt(a, b):
    return lax.dot_general(a, b, (((1,), (1,)), ((), ())), preferred_element_type=F32)


def _sigmoid(x):
    return 0.5 * jnp.tanh(0.5 * x) + 0.5


def _const_spec(shape):
    nd = len(shape)
    return pl.BlockSpec(shape, lambda *_: (0,) * nd, pipeline_mode=pl.Buffered(1))


def _bias_from_distance(d, rb_ref, h, b_far):
    val = jnp.zeros(d.shape, F32)
    for b in range(NUM_BUCKETS - 2, -1, -1):
        val = jnp.where(d < BUCKET_THR[b + 1], rb_ref[b, h] - b_far, val)
    return val


def _bias_kernel(rb_ref, tile_ref, row_ref, own_ref):
    hp = pl.program_id(0)
    j = lax.broadcasted_iota(jnp.int32, (BLOCK, BLOCK), 0)
    i = lax.broadcasted_iota(jnp.int32, (BLOCK, BLOCK), 1)
    jr = lax.broadcasted_iota(jnp.int32, (1, BLOCK), 1)
    for e in range(2):
        h = 2 * hp + e
        b_far = rb_ref[NUM_BUCKETS - 1, h]
        lanes = slice(e * BLOCK, (e + 1) * BLOCK)
        tile_ref[0, 0, :, lanes] = jnp.where(j <= i, _bias_from_distance(i - j, rb_ref, h, b_far) * LOG2E, NEG_BIG)
        tile_ref[0, 1, :, lanes] = _bias_from_distance(i - j + BLOCK, rb_ref, h, b_far) * LOG2E
        row_ref[e] = _bias_from_distance(BLOCK - jr, rb_ref, h, b_far)
        own_ref[e] = jnp.zeros((1, 128), F32) + (rb_ref[0, h] - b_far)
    tile_ref[0, 2] = jnp.zeros((BLOCK, 2 * BLOCK), F32)


def _bias_tables(rel_bias):
    n_hp = N_HEADS // 2
    return pl.pallas_call(
        _bias_kernel,
        grid=(n_hp,),
        in_specs=[pl.BlockSpec(memory_space=pltpu.SMEM)],
        out_specs=[
            pl.BlockSpec((1, 3, BLOCK, 2 * BLOCK), lambda hp: (hp, 0, 0, 0)),
            pl.BlockSpec((2, 1, BLOCK), lambda hp: (hp, 0, 0)),
            pl.BlockSpec((2, 1, 128), lambda hp: (hp, 0, 0)),
        ],
        out_shape=[
            jax.ShapeDtypeStruct((n_hp, 3, BLOCK, 2 * BLOCK), F32),
            jax.ShapeDtypeStruct((N_HEADS, 1, BLOCK), F32),
            jax.ShapeDtypeStruct((N_HEADS, 1, 128), F32),
        ],
    )(rel_bias)


def _inproj_kernel(*refs, tm, sample, tiles_per_seq):
    if sample:
        (x_ref, n1_ref, w_ref, qn_ref, kn_ref, bd_ref, cw_ref, st0_ref, st1_ref,
         q_ref, k_ref, v_ref, sin_ref, ga_ref, gb_ref, u_ref) = refs
    else:
        (x_ref, n1_ref, w_ref, qn_ref, kn_ref, bd_ref, cw_ref,
         q_ref, kt_ref, vt_ref, kb_ref, vtb_ref, ksum_ref, sin_ref, ga_ref, gb_ref, cst_ref,
         ucar_ref) = refs

    x = x_ref[...]
    ms = jnp.mean(x * x, axis=-1, keepdims=True)
    xn = (x * lax.rsqrt(ms + EPS) * n1_ref[...]).astype(BF16)

    def proj(a, b):
        return _dot(xn, w_ref[:, a:b])

    def head_norm(t, w_row):
        sq = (t * t).astype(BF16)
        bd = bd_ref[...]
        half = ATTN_W // 2
        msq = jnp.concatenate([_dot(sq[:, :half], bd), _dot(sq[:, half:], bd)], axis=1)
        return t * lax.rsqrt(msq + EPS) * w_row

    q = head_norm(proj(0, ATTN_W), qn_ref[...]) * (HEAD_DIM ** -0.5)
    k = head_norm(proj(ATTN_W, 2 * ATTN_W), kn_ref[...])
    v = proj(2 * ATTN_W, 3 * ATTN_W)
    c0 = 3 * ATTN_W
    cb = proj(c0, c0 + CONV_CH)
    u = proj(c0 + CONV_CH, c0 + 2 * CONV_CH) * proj(c0 + 2 * CONV_CH, c0 + 3 * CONV_CH)
    g0 = c0 + 3 * CONV_CH
    ga_ref[...] = proj(g0, g0 + D_MODEL).astype(ga_ref.dtype)
    gb_ref[...] = proj(g0 + D_MODEL, g0 + 2 * D_MODEL).astype(gb_ref.dtype)

    w0, w1, w2 = cw_ref[0:1, :], cw_ref[1:2, :], cw_ref[2:3, :]
    if sample:
        q_ref[...] = q
        k_ref[...] = k
        v_ref[...] = v
        uc = st0_ref[...] * w0 + st1_ref[...] * w1 + u * w2
        u_ref[...] = u
    else:
        q_ref[...] = (q * LOG2E).astype(BF16)
        kb_ref[...] = k.astype(BF16)
        kt_ref[0] = k.T
        v_t = v.T
        vt_ref[0] = v_t
        for r in range(tm // BLOCK):
            vtb_ref[0, r] = v_t[:, r * BLOCK:(r + 1) * BLOCK].astype(BF16)
            ksum_ref[0, r:r + 1, :] = jnp.sum(k[r * BLOCK:(r + 1) * BLOCK], axis=0, keepdims=True)

        @pl.when(pl.program_id(0) % tiles_per_seq == 0)
        def _():
            ucar_ref[...] = jnp.zeros_like(ucar_ref)

        ue = jnp.concatenate([ucar_ref[...], u], axis=0)
        uc = ue[6:6 + tm] * w0 + ue[7:7 + tm] * w1 + u * w2
        ucar_ref[...] = u[tm - 8:tm]
        cst_ref[0] = u[tm - 2:tm]
    sin_ref[...] = (cb * uc).astype(sin_ref.dtype)


def _inproj(x, n1, w_in, qn, kn, bd, conv_w, states, *, sample, seq_len=1):
    n_tok = x.shape[0]
    tm = n_tok if sample else TOKEN_TILE
    n_tiles = n_tok // tm
    tiles_per_seq = 1 if sample else seq_len // tm

    def row(width):
        return pl.BlockSpec((tm, width), lambda i, *_: (i, 0))

    in_specs = [row(D_MODEL), _const_spec((1, D_MODEL)), _const_spec((D_MODEL, N_COLS)),
                _const_spec((1, ATTN_W)), _const_spec((1, ATTN_W)), _const_spec((BLOCK, BLOCK)),
                _const_spec((3, CONV_CH))]
    args = [x, n1, w_in, qn, kn, bd, conv_w]
    if sample:
        in_specs += [row(CONV_CH), row(CONV_CH)]
        args += list(states)
        out_specs = [row(ATTN_W), row(ATTN_W), row(ATTN_W), row(CONV_CH), row(D_MODEL), row(D_MODEL),
                     row(CONV_CH)]
        out_shape = [jax.ShapeDtypeStruct((n_tok, ATTN_W), F32)] * 3 + [
            jax.ShapeDtypeStruct((n_tok, CONV_CH), BF16),
            jax.ShapeDtypeStruct((n_tok, D_MODEL), BF16),
            jax.ShapeDtypeStruct((n_tok, D_MODEL), BF16),
            jax.ShapeDtypeStruct((n_tok, CONV_CH), F32)]
        scratch = []
    else:
        blocks_per_tile = tm // BLOCK
        n_seq = n_tiles // tiles_per_seq
        t_spec = pl.BlockSpec((1, ATTN_W, tm), lambda i, *_: (i // tiles_per_seq, 0, i % tiles_per_seq))
        out_specs = [row(ATTN_W), t_spec, t_spec, row(ATTN_W),
                     pl.BlockSpec((1, blocks_per_tile, ATTN_W, BLOCK),
                                  lambda i, *_: (i // tiles_per_seq, i % tiles_per_seq, 0, 0)),
                     pl.BlockSpec((1, blocks_per_tile, ATTN_W), lambda i, *_: (i, 0, 0)),
                     row(CONV_CH), row(D_MODEL), row(D_MODEL),
                     pl.BlockSpec((1, 2, CONV_CH), lambda i, *_: (i // tiles_per_seq, 0, 0))]
        out_shape = [jax.ShapeDtypeStruct((n_tok, ATTN_W), BF16),
                     jax.ShapeDtypeStruct((n_seq, ATTN_W, seq_len), F32),
                     jax.ShapeDtypeStruct((n_seq, ATTN_W, seq_len), F32),
                     jax.ShapeDtypeStruct((n_tok, ATTN_W), BF16),
                     jax.ShapeDtypeStruct((n_seq, seq_len // BLOCK, ATTN_W, BLOCK), BF16),
                     jax.ShapeDtypeStruct((n_tiles, blocks_per_tile, ATTN_W), F32),
                     jax.ShapeDtypeStruct((n_tok, CONV_CH), BF16),
                     jax.ShapeDtypeStruct((n_tok, D_MODEL), BF16),
                     jax.ShapeDtypeStruct((n_tok, D_MODEL), BF16),
                     jax.ShapeDtypeStruct((n_tiles // tiles_per_seq, 2, CONV_CH), F32)]
        scratch = [pltpu.VMEM((8, CONV_CH), F32)]
    return pl.pallas_call(
        functools.partial(_inproj_kernel, tm=tm, sample=sample, tiles_per_seq=tiles_per_seq),
        grid=(n_tiles,),
        in_specs=in_specs,
        out_specs=out_specs,
        scratch_shapes=scratch,
        out_shape=out_shape,
        compiler_params=pltpu.CompilerParams(dimension_semantics=("arbitrary",),
                                             vmem_limit_bytes=VMEM_LIMIT),
    )(*args)


STREAM_SLOTS = 5


def _query_block_order(t, n_blocks):
    return jnp.where(t % 2 == 0, t // 2, n_blocks - 1 - t // 2)
FFN_STREAM_PAGES = 32


def _block_diag_query(q_row):
    row = lax.broadcasted_iota(jnp.int32, (N_HEADS, ATTN_W), 0)
    lane_head = lax.broadcasted_iota(jnp.int32, (N_HEADS, ATTN_W), 1) // HEAD_DIM
    q_rows = jnp.where(row == lane_head, q_row, 0.0)
    q_hi = q_rows.astype(BF16).astype(F32)
    return jnp.concatenate([q_hi, q_rows - q_hi], axis=0).astype(BF16)


def _page_scores(q2, page):
    s2 = _dot(q2, page.reshape(ATTN_W, PAGE).astype(BF16))
    return s2[:N_HEADS] + s2[N_HEADS:]


def _attn_kernel(pt_ref, q_ref, k_ref, vt_ref, ksum_ref, bias_ref, qs_ref, ck_ref, o_ref, sc_ref,
                 pen_ref, m_ref, l_ref, alpha_ref, acc_ref, s_ref, p_ref, pbuf, psem, *, pages_per_step, n_pages):
    n_blocks = k_ref.shape[1] // BLOCK
    qb = _query_block_order(pl.program_id(2), n_blocks)
    nq = 2 * BLOCK
    step = (pl.program_id(0) * pl.num_programs(1) + pl.program_id(1)) * pl.num_programs(2) + pl.program_id(2)
    n_steps = pl.num_programs(0) * pl.num_programs(1) * pl.num_programs(2)

    def page_copies(step_, slot_):
        return [pltpu.make_async_copy(ck_ref.at[0, pt_ref[step_ * pages_per_step + i]], pbuf.at[slot_, i],
                                      psem.at[slot_]) for i in range(pages_per_step)]

    @pl.when(step == 0)
    def _():
        for ahead in range(STREAM_SLOTS - 1):
            for c in page_copies(ahead, ahead):
                c.start()

    @pl.when(step + (STREAM_SLOTS - 1) < n_steps)
    def _():
        nxt = step + (STREAM_SLOTS - 1)
        for c in page_copies(nxt, nxt % STREAM_SLOTS):
            c.start()

    slot = step % STREAM_SLOTS
    pltpu.make_async_copy(ck_ref.at[0, pl.ds(0, pages_per_step)], pbuf.at[slot], psem.at[slot]).wait()
    page0 = step * pages_per_step
    seq0 = page0 // n_pages
    seq1 = jnp.minimum(seq0 + 1, qs_ref.shape[0] - 1)
    q2s0 = _block_diag_query(qs_ref[pl.ds(seq0, 1), :])
    q2s1 = _block_diag_query(qs_ref[pl.ds(seq1, 1), :])

    def score_page(i):
        q2s = jnp.where((page0 + i) // n_pages == seq0, q2s0, q2s1)
        sc_ref[i] = _page_scores(q2s, pbuf[slot, i])

    pages_at_head = (pages_per_step * 5) // 8
    for i in range(pages_at_head):
        score_page(i)

    q = q_ref[0]
    lane_head = lax.broadcasted_iota(jnp.int32, q.shape, 1) // HEAD_DIM
    zero = jnp.zeros_like(q)
    q2 = jnp.concatenate([jnp.where(lane_head == 0, q, zero), jnp.where(lane_head == 1, q, zero)], axis=0)

    kmean = ksum_ref[0] * (1.0 / BLOCK)
    km_hi = kmean.astype(BF16)
    km_lo = (kmean - km_hi.astype(F32)).astype(BF16)
    g2 = _dot_nt(jnp.concatenate([km_hi, km_lo], axis=0), q2)
    g = g2[:n_blocks] + g2[n_blocks:]
    blk = lax.broadcasted_iota(jnp.int32, (n_blocks, nq), 0)
    valid = blk < qb
    g = jnp.where(valid, g, -jnp.inf)
    rank = jnp.zeros(g.shape, jnp.int32)
    for j in range(n_blocks):
        gj = g[j:j + 1, :]
        beats = (gj > g) | ((gj == g) & (blk > j))
        rank = rank + beats.astype(jnp.int32)
    sel = valid & (rank < TOP_K)
    pen_ref[0:n_blocks] = jnp.where(sel, 0.0, NEG_BIG)
    pen_ref[n_blocks:2 * n_blocks] = jnp.zeros((n_blocks, nq), F32)

    def item(j):
        own = j == 0
        kb = jnp.where(own, qb, j - 1)
        tile = jnp.where(own, 0, jnp.where(j == qb, 1, 2))
        pen_row = jnp.where(own, n_blocks, j - 1)
        return kb, tile, pen_row

    def item_scores(j):
        kb, tile, pen_row = item(j)
        kblk = k_ref[0, pl.ds(pl.multiple_of(kb * BLOCK, BLOCK), BLOCK), :]
        return _dot_nt(kblk, q2) + pen_ref[pl.ds(pen_row, 1), :] + bias_ref[0, tile]

    def pv(kb, p):
        vt = vt_ref[0, kb]
        return jnp.concatenate([_dot(vt[:HEAD_DIM], p[:, :BLOCK]), _dot(vt[HEAD_DIM:], p[:, BLOCK:])], axis=1)

    n_items = qb + 1
    s_ref[...] = item_scores(0)
    m_ref[...] = jnp.full(m_ref.shape, NEG_BIG, F32)
    l_ref[...] = jnp.zeros(l_ref.shape, F32)
    alpha_ref[...] = jnp.ones(alpha_ref.shape, F32)
    acc_ref[...] = jnp.zeros(acc_ref.shape, F32)
    p_ref[...] = jnp.zeros(p_ref.shape, BF16)

    def trip(j, carry):
        s = s_ref[...]
        s_next = item_scores(jnp.minimum(j + 1, n_items - 1))
        kb_prev, _, _ = item(jnp.maximum(j - 1, 0))
        acc = alpha_ref[...] * acc_ref[...] + pv(kb_prev, p_ref[...])
        m_old = m_ref[...]
        m_new = jnp.maximum(m_old, jnp.max(s, axis=0, keepdims=True))
        alpha = jnp.exp2(m_old - m_new)
        p = jnp.exp2(s - m_new)
        l_ref[...] = alpha * l_ref[...] + jnp.sum(p, axis=0, keepdims=True)
        m_ref[...] = m_new
        acc_ref[...] = acc
        alpha_ref[...] = alpha
        p_ref[...] = p.astype(BF16)
        s_ref[...] = s_next
        return carry

    lax.fori_loop(0, n_items, trip, 0)
    kb_last, _, _ = item(qb)
    acc = alpha_ref[...] * acc_ref[...] + pv(kb_last, p_ref[...])
    out = acc / l_ref[...]
    out_t = jnp.concatenate([out[:, :BLOCK], out[:, BLOCK:]], axis=0)
    o_ref[0] = out_t.T.astype(o_ref.dtype)

    for i in range(pages_at_head, pages_per_step):
        score_page(i)


def _prompt_attention(q, k, vt_blocks, ksum, bias_tiles, page_table_flat, q_sample, cache_kt, n_pages,
                      n_stream_pages):
    b, s, _ = q.shape
    n_blocks = s // BLOCK
    n_hp = N_HEADS // 2
    n_steps = b * n_hp * n_blocks
    n_seq = q_sample.shape[0]
    pages_per_step = n_stream_pages // n_steps
    assert pages_per_step * n_steps == n_stream_pages and pages_per_step <= n_pages

    def step_of(bi, hp, qb):
        return (bi * n_hp + hp) * n_blocks + qb

    attn, scores = pl.pallas_call(
        functools.partial(_attn_kernel, pages_per_step=pages_per_step, n_pages=n_pages),
        grid_spec=pltpu.PrefetchScalarGridSpec(
            num_scalar_prefetch=1,
            grid=(b, n_hp, n_blocks),
            in_specs=[
                pl.BlockSpec((1, BLOCK, 128), lambda bi, hp, qb, pt: (bi, _query_block_order(qb, n_blocks), hp)),
                pl.BlockSpec((1, s, 128), lambda bi, hp, qb, pt: (bi, 0, hp)),
                pl.BlockSpec((1, n_blocks, 128, BLOCK), lambda bi, hp, qb, pt: (bi, 0, hp, 0)),
                pl.BlockSpec((1, n_blocks, 128), lambda bi, hp, qb, pt: (bi, 0, hp)),
                pl.BlockSpec((1, 3, BLOCK, 2 * BLOCK), lambda bi, hp, qb, pt: (hp, 0, 0, 0)),
                pl.BlockSpec((n_seq, ATTN_W), lambda bi, hp, qb, pt: (0, 0), pipeline_mode=pl.Buffered(1)),
                pl.BlockSpec(memory_space=pl.ANY),
            ],
            out_specs=[
                pl.BlockSpec((1, BLOCK, 128), lambda bi, hp, qb, pt: (bi, _query_block_order(qb, n_blocks), hp)),
                pl.BlockSpec((pages_per_step, N_HEADS, PAGE), lambda bi, hp, qb, pt: (step_of(bi, hp, qb), 0, 0)),
            ],
            scratch_shapes=[
                pltpu.VMEM((2 * n_blocks, 2 * BLOCK), F32),
                pltpu.VMEM((1, 2 * BLOCK), F32),
                pltpu.VMEM((1, 2 * BLOCK), F32),
                pltpu.VMEM((1, 2 * BLOCK), F32),
                pltpu.VMEM((HEAD_DIM, 2 * BLOCK), F32),
                pltpu.VMEM((BLOCK, 2 * BLOCK), F32),
                pltpu.VMEM((BLOCK, 2 * BLOCK), BF16),
                pltpu.VMEM((STREAM_SLOTS, pages_per_step, N_HEADS, HEAD_DIM, PAGE), F32),
                pltpu.SemaphoreType.DMA((STREAM_SLOTS,)),
            ],
        ),
        out_shape=[jax.ShapeDtypeStruct((b, s, ATTN_W), BF16),
                   jax.ShapeDtypeStruct((n_stream_pages, N_HEADS, PAGE), F32)],
        compiler_params=pltpu.CompilerParams(
            dimension_semantics=("arbitrary", "arbitrary", "arbitrary"),
            vmem_limit_bytes=VMEM_LIMIT),
    )(page_table_flat, q, k, vt_blocks, ksum, bias_tiles, q_sample, cache_kt)
    return attn, scores


def _merge_ffn_kernel(*refs, tm, sample, tiles_per_seq, stream):
    if sample:
        (x_ref, at_ref, sin_ref, ga_ref, gb_ref, wa_ref, wc_ref, wo_ref, n2_ref, wup_ref, fcw_ref,
         wdn_ref, fs0_ref, fs1_ref, y_ref, g_ref, acc_ref) = refs
    else:
        (pt_ref, x_ref, at_ref, sin_ref, ga_ref, gb_ref, wa_ref, wc_ref, wo_ref, n2_ref, wup_ref, fcw_ref,
         wdn_ref, qs_ref, ck_ref, y_ref, fst_ref, sc_ref, acc_ref, gcar_ref, pbuf, psem) = refs
        first_page, pages_per_step = stream
        page0 = first_page + pl.program_id(0) * pages_per_step
        for i in range(pages_per_step):
            pltpu.make_async_copy(ck_ref.at[0, pt_ref[page0 + i]], pbuf.at[i], psem.at[0]).start()

    a = _dot(at_ref[...].astype(BF16), wa_ref[...])
    b = _dot(sin_ref[...], wc_ref[...])
    merged = _sigmoid(ga_ref[...].astype(F32)) * a + _sigmoid(gb_ref[...].astype(F32)) * b
    x1 = x_ref[...] + _dot(merged.astype(BF16), wo_ref[...])
    ms = jnp.mean(x1 * x1, axis=-1, keepdims=True)
    xn2 = (x1 * lax.rsqrt(ms + EPS) * n2_ref[...]).astype(BF16)
    acc_ref[...] = x1

    if not sample:
        @pl.when(pl.program_id(0) % tiles_per_seq == 0)
        def _():
            gcar_ref[...] = jnp.zeros_like(gcar_ref)

    for c in range(N_FF_CHUNKS):
        cols = slice(c * FF_CHUNK, (c + 1) * FF_CHUNK)
        g = _dot(xn2, wup_ref[:, cols])
        u = _dot(xn2, wup_ref[:, D_FF + c * FF_CHUNK:D_FF + (c + 1) * FF_CHUNK])
        w0, w1, w2 = fcw_ref[0:1, cols], fcw_ref[1:2, cols], fcw_ref[2:3, cols]
        if sample:
            gc = fs0_ref[:, cols] * w0 + fs1_ref[:, cols] * w1 + g * w2
            g_ref[:, cols] = g
        else:
            ge = jnp.concatenate([gcar_ref[:, cols], g], axis=0)
            gc = ge[6:6 + tm] * w0 + ge[7:7 + tm] * w1 + g * w2
            gcar_ref[:, cols] = g[tm - 8:tm]
            fst_ref[0, :, cols] = g[tm - 2:tm]
        hidden = (gc * _sigmoid(gc) * u).astype(BF16)
        acc_ref[...] += _dot(hidden, wdn_ref[cols, :])

        if not sample and c == N_FF_CHUNKS // 2:
            pltpu.make_async_copy(ck_ref.at[0, pl.ds(0, pages_per_step)], pbuf, psem.at[0]).wait()
            q2s = _block_diag_query(qs_ref[0])
            for i in range(pages_per_step):
                sc_ref[i] = _page_scores(q2s, pbuf[i])
    y_ref[...] = acc_ref[...]


def _merge_ffn(x, attn, s_in, ga, gb, wa, wc, wo, n2, wup, fcw, wdn, states, *, sample, seq_len=1, stream=None):
    n_tok = x.shape[0]
    tm = n_tok if sample else TOKEN_TILE
    n_tiles = n_tok // tm
    tiles_per_seq = 1 if sample else seq_len // tm

    def row(width):
        return pl.BlockSpec((tm, width), lambda i, *_: (i, 0))

    in_specs = [row(D_MODEL), row(ATTN_W), row(CONV_CH), row(D_MODEL), row(D_MODEL),
                _const_spec((ATTN_W, D_MODEL)), _const_spec((CONV_CH, D_MODEL)),
                _const_spec((D_MODEL, D_MODEL)), _const_spec((1, D_MODEL)),
                _const_spec((D_MODEL, 2 * D_FF)), _const_spec((3, D_FF)), _const_spec((D_FF, D_MODEL))]
    args = [x, attn, s_in, ga, gb, wa, wc, wo, n2, wup, fcw, wdn]
    scratch = [pltpu.VMEM((tm, D_MODEL), F32)]
    if sample:
        in_specs += [row(D_FF), row(D_FF)]
        args += list(states)
        out_specs = [row(D_MODEL), row(D_FF)]
        out_shape = [jax.ShapeDtypeStruct((n_tok, D_MODEL), F32),
                     jax.ShapeDtypeStruct((n_tok, D_FF), F32)]
    else:
        n_seq = n_tiles // tiles_per_seq
        page_table_flat, q_sample, cache_kt, n_pages, first_page = stream
        pps = FFN_STREAM_PAGES
        assert first_page % n_pages == 0 and n_pages % pps == 0
        in_specs += [pl.BlockSpec((1, 1, ATTN_W), lambda i, *_: ((first_page + i * pps) // n_pages, 0, 0)),
                     pl.BlockSpec(memory_space=pl.ANY)]
        args = [page_table_flat] + args + [q_sample, cache_kt]
        out_specs = [row(D_MODEL), pl.BlockSpec((1, 2, D_FF), lambda i, *_: (i // tiles_per_seq, 0, 0)),
                     pl.BlockSpec((pps, N_HEADS, PAGE), lambda i, *_: (i, 0, 0))]
        out_shape = [jax.ShapeDtypeStruct((n_tok, D_MODEL), F32),
                     jax.ShapeDtypeStruct((n_seq, 2, D_FF), F32),
                     jax.ShapeDtypeStruct((n_tiles * pps, N_HEADS, PAGE), F32)]
        scratch += [pltpu.VMEM((8, D_FF), F32),
                    pltpu.VMEM((pps, N_HEADS, HEAD_DIM, PAGE), F32),
                    pltpu.SemaphoreType.DMA((1,))]
    return pl.pallas_call(
        functools.partial(_merge_ffn_kernel, tm=tm, sample=sample, tiles_per_seq=tiles_per_seq,
                          stream=None if sample else (stream[4], FFN_STREAM_PAGES)),
        grid_spec=pltpu.PrefetchScalarGridSpec(
            num_scalar_prefetch=0 if sample else 1,
            grid=(n_tiles,),
            in_specs=in_specs,
            out_specs=out_specs,
            scratch_shapes=scratch,
        ),
        out_shape=out_shape,
        compiler_params=pltpu.CompilerParams(dimension_semantics=("arbitrary",),
                                             vmem_limit_bytes=VMEM_LIMIT),
    )(*args)


SEQ_PER_SELECT_STEP = 8


def _select_kernel(sc_ref, idx_ref):
    n_pages = sc_ref.shape[1]
    pages_per_block = BLOCK // PAGE
    n_blocks = n_pages // pages_per_block
    blk = lax.broadcasted_iota(jnp.int32, (n_blocks, N_HEADS, 128), 0)
    for s in range(SEQ_PER_SELECT_STEP):
        page_sum = jnp.sum(sc_ref[s], axis=-1, keepdims=True)
        block_sum = jnp.sum(page_sum.reshape(n_blocks, pages_per_block, N_HEADS, 1), axis=1)
        gate = jnp.broadcast_to(block_sum * (1.0 / BLOCK), (n_blocks, N_HEADS, 128))
        for r in range(TOP_K):
            best = jnp.max(gate, axis=0, keepdims=True)
            idx = jnp.min(jnp.where(gate == best, blk, n_blocks), axis=0, keepdims=True)
            idx_ref[s, r] = idx[0]
            gate = jnp.where(blk == idx, -jnp.inf, gate)


def _select_blocks(scores):
    n_seq, n_pages = scores.shape[:2]
    assert n_seq % SEQ_PER_SELECT_STEP == 0
    return pl.pallas_call(
        _select_kernel,
        grid=(n_seq // SEQ_PER_SELECT_STEP,),
        in_specs=[pl.BlockSpec((SEQ_PER_SELECT_STEP, n_pages, N_HEADS, PAGE), lambda i: (i, 0, 0, 0))],
        out_specs=pl.BlockSpec((SEQ_PER_SELECT_STEP, TOP_K, N_HEADS, 128), lambda i: (i, 0, 0, 0)),
        out_shape=jax.ShapeDtypeStruct((n_seq, TOP_K, N_HEADS, 128), jnp.int32),
    )(scores)


PAGES_PER_BLOCK = BLOCK // PAGE
CHUNKS_PER_HEAD = TOP_K * PAGES_PER_BLOCK
N_CHUNKS = N_HEADS * CHUNKS_PER_HEAD
KEYS_PER_HEAD = CHUNKS_PER_HEAD * PAGE


SEQ_PER_ATTN_STEP = 8


def _sample_attn_kernel(pt_ref, idx_ref, sc0_ref, sc1_ref, q_ref, kn_ref, vn_ref, brow_ref, ownb_ref,
                        cv_ref, o_ref, vbuf, logit_ref, sem, *, n_steps, n_pages, n_seq_first):
    g = pl.program_id(0)
    slot = g % 2

    def picked_page(seq, h, r, j):
        n = idx_ref[(seq * TOP_K + r) * N_HEADS + h]
        return n, n * PAGES_PER_BLOCK + j

    def copies(step_, slot_):
        out = []
        for s in range(SEQ_PER_ATTN_STEP):
            seq = step_ * SEQ_PER_ATTN_STEP + s
            for h in range(N_HEADS):
                for r in range(TOP_K):
                    for j in range(PAGES_PER_BLOCK):
                        _, logical = picked_page(seq, h, r, j)
                        page = pt_ref[seq * n_pages + logical]
                        c = s * N_CHUNKS + (h * TOP_K + r) * PAGES_PER_BLOCK + j
                        out.append(pltpu.make_async_copy(cv_ref.at[0, page, h], vbuf.at[slot_, c], sem.at[slot_]))
        return out

    @pl.when(g == 0)
    def _():
        for c in copies(0, 0):
            c.start()

    @pl.when(g + 1 < n_steps)
    def _():
        for c in copies(g + 1, 1 - slot):
            c.start()

    last_block = n_pages // PAGES_PER_BLOCK - 1
    col = lax.broadcasted_iota(jnp.int32, (N_HEADS, N_HEADS * KEYS_PER_HEAD), 1)
    row = lax.broadcasted_iota(jnp.int32, (N_HEADS, N_HEADS * KEYS_PER_HEAD), 0)
    own_keys = col // KEYS_PER_HEAD == row
    per_seq = []
    for s in range(SEQ_PER_ATTN_STEP):
        seq = g * SEQ_PER_ATTN_STEP + s
        for h in range(N_HEADS):
            for r in range(TOP_K):
                for j in range(PAGES_PER_BLOCK):
                    n, logical = picked_page(seq, h, r, j)
                    tile = jnp.where(seq < n_seq_first, sc0_ref[s, logical], sc1_ref[s, logical])
                    near = brow_ref[h][:, j * PAGE:(j + 1) * PAGE]
                    piece = tile[h:h + 1, :] + jnp.where(n == last_block, near, jnp.zeros_like(near))
                    c = r * PAGES_PER_BLOCK + j
                    logit_ref[s, h:h + 1, c * PAGE:(c + 1) * PAGE] = piece
        logits = logit_ref[s]

        q = q_ref[s]
        own = jnp.sum(q * kn_ref[s], axis=-1, keepdims=True) + ownb_ref[:, 0, 0:1]
        m = jnp.maximum(jnp.max(logits, axis=-1, keepdims=True), own)
        p = jnp.exp(logits - m)
        p_own = jnp.exp(own - m)
        denom = jnp.sum(p, axis=-1, keepdims=True) + p_own

        p_wide = jnp.where(own_keys, jnp.concatenate([p] * N_HEADS, axis=1), 0.0)
        p16 = jnp.concatenate([p_wide, jnp.zeros_like(p_wide)], axis=0).astype(BF16)
        per_seq.append((p16, p_own, denom))

    pltpu.make_async_copy(cv_ref.at[0, pl.ds(0, SEQ_PER_ATTN_STEP * N_CHUNKS), 0], vbuf.at[slot],
                          sem.at[slot]).wait()
    for s, (p16, p_own, denom) in enumerate(per_seq):
        vt = jnp.concatenate([vbuf[slot, s * N_CHUNKS + c] for c in range(N_CHUNKS)], axis=1).astype(BF16)
        ctx = _dot_nt(p16, vt)[:N_HEADS] + p_own * vn_ref[s]
        o_ref[s] = ctx / denom


def _sample_attention(page_table_flat, idx_flat, score_parts, q_heads, k_new, v_new, brow, ownb, cache_vt, n_pages):
    n_seq = q_heads.shape[0]
    sps = SEQ_PER_ATTN_STEP
    n_first = score_parts[0].shape[0]
    assert n_seq % sps == 0 and n_first % sps == 0
    n_steps = n_seq // sps
    head_spec = pl.BlockSpec((sps, N_HEADS, HEAD_DIM), lambda g, pt, ix: (g, 0, 0))
    starts = [0, n_first // sps]

    def part_spec(k):
        last = score_parts[k].shape[0] // sps - 1
        return pl.BlockSpec((sps, n_pages, N_HEADS, PAGE),
                            lambda g, pt, ix: (jnp.clip(g - starts[k], 0, last), 0, 0, 0))

    return pl.pallas_call(
        functools.partial(_sample_attn_kernel, n_steps=n_steps, n_pages=n_pages, n_seq_first=n_first),
        grid_spec=pltpu.PrefetchScalarGridSpec(
            num_scalar_prefetch=2,
            grid=(n_steps,),
            in_specs=[part_spec(0), part_spec(1),
                      head_spec, head_spec, head_spec,
                      pl.BlockSpec((N_HEADS, 1, BLOCK), lambda g, pt, ix: (0, 0, 0)),
                      pl.BlockSpec((N_HEADS, 1, 128), lambda g, pt, ix: (0, 0, 0)),
                      pl.BlockSpec(memory_space=pl.ANY)],
            out_specs=head_spec,
            scratch_shapes=[pltpu.VMEM((2, sps * N_CHUNKS, HEAD_DIM, PAGE), F32),
                            pltpu.VMEM((sps, N_HEADS, KEYS_PER_HEAD), F32),
                            pltpu.SemaphoreType.DMA((2,))],
        ),
        out_shape=jax.ShapeDtypeStruct((n_seq, N_HEADS, HEAD_DIM), F32),
        compiler_params=pltpu.CompilerParams(dimension_semantics=("arbitrary",),
                                             vmem_limit_bytes=VMEM_LIMIT),
    )(page_table_flat, idx_flat, *score_parts, q_heads, k_new, v_new, brow, ownb, cache_vt)


def kernel(x_prompt, x_sample, cache_k, cache_v, page_table, state_conv, state_ffn, norm1_w, w_in, q_norm_w, k_norm_w, conv_w, w_attn_up, w_conv_out, w_o, norm2_w, w_ffn_up, ffn_conv_w, w_ffn_down, rel_bias):
    batch, seq, _ = x_prompt.shape
    n_seq = x_sample.shape[0]
    n_pages = page_table.shape[1]

    w_in_b = w_in[0].astype(BF16)
    wa = w_attn_up[0].astype(BF16)
    wc = w_conv_out[0].astype(BF16)
    wo = w_o[0].astype(BF16)
    wup = w_ffn_up[0].astype(BF16)
    wdn = w_ffn_down[0].astype(BF16)
    fcw = ffn_conv_w[0]
    n1 = norm1_w[0].reshape(1, D_MODEL)
    n2 = norm2_w[0].reshape(1, D_MODEL)
    qn = jnp.tile(q_norm_w[0], N_HEADS).reshape(1, ATTN_W)
    kn = jnp.tile(k_norm_w[0], N_HEADS).reshape(1, ATTN_W)
    lane = np.arange(BLOCK)
    bd = jnp.asarray((lane[:, None] // HEAD_DIM == lane[None, :] // HEAD_DIM) / HEAD_DIM, BF16)
    cw = conv_w[0]

    bias_tiles, bias_row, bias_own = _bias_tables(rel_bias)

    xs = x_sample.reshape(n_seq, D_MODEL)
    (q_s, k_s, v_s, sin_s, ga_s, gb_s, u_s) = _inproj(
        xs, n1, w_in_b, qn, kn, bd, cw, (state_conv[0, :, 0, :], state_conv[0, :, 1, :]), sample=True)
    pt_flat = page_table.reshape(-1)
    cache_kt = cache_k.transpose(0, 1, 3, 4, 2)
    cache_vt = cache_v.transpose(0, 1, 3, 4, 2)
    xp = x_prompt.reshape(batch * seq, D_MODEL)
    n_attn_pages = n_seq * n_pages - (batch * seq // TOKEN_TILE) * FFN_STREAM_PAGES
    (q_b, kt_p, vt_p, k_b, vt_b, ksum_p, sin_p, ga_p, gb_p, conv_p) = _inproj(
        xp, n1, w_in_b, qn, kn, bd, cw, None, sample=False, seq_len=seq)

    attn_p, scores_0 = _prompt_attention(q_b.reshape(batch, seq, ATTN_W), k_b.reshape(batch, seq, ATTN_W), vt_b,
                                         ksum_p.reshape(batch, seq // BLOCK, ATTN_W), bias_tiles,
                                         pt_flat, q_s, cache_kt, n_pages, n_attn_pages)
    y_p, ffn_p, scores_1 = _merge_ffn(xp, attn_p.reshape(batch * seq, ATTN_W), sin_p, ga_p, gb_p,
                                      wa, wc, wo, n2, wup, fcw, wdn, None, sample=False, seq_len=seq,
                                      stream=(pt_flat, q_s.reshape(n_seq, 1, ATTN_W), cache_kt, n_pages, n_attn_pages))

    score_parts = [sc.reshape(-1, n_pages, N_HEADS, PAGE) for sc in (scores_0, scores_1)]
    idx = jnp.concatenate([_select_blocks(sc)[..., 0] for sc in score_parts], axis=0)
    attn_s = _sample_attention(pt_flat, idx.reshape(-1), score_parts, q_s.reshape(n_seq, N_HEADS, HEAD_DIM),
                               k_s.reshape(n_seq, N_HEADS, HEAD_DIM), v_s.reshape(n_seq, N_HEADS, HEAD_DIM),
                               bias_row, bias_own, cache_vt, n_pages)

    y_s, g_s = _merge_ffn(xs, attn_s.reshape(n_seq, ATTN_W), sin_s, ga_s, gb_s,
                          wa, wc, wo, n2, wup, fcw, wdn,
                          (state_ffn[0, :, 0, :], state_ffn[0, :, 1, :]), sample=True)

    return (
        y_p.reshape(batch, seq, D_MODEL),
        y_s.reshape(n_seq, 1, D_MODEL),
        kt_p.reshape(1, batch, N_HEADS, HEAD_DIM, seq).transpose(0, 1, 4, 2, 3),
        vt_p.reshape(1, batch, N_HEADS, HEAD_DIM, seq).transpose(0, 1, 4, 2, 3),
        conv_p[None],
        ffn_p[None],
        k_s.reshape(1, n_seq, 1, N_HEADS, HEAD_DIM),
        v_s.reshape(1, n_seq, 1, N_HEADS, HEAD_DIM),
        jnp.stack([state_conv[0, :, 1, :], u_s], axis=1)[None],
        jnp.stack([state_ffn[0, :, 1, :], g_s], axis=1)[None],
    )
```
